```python
import math
import jax, jax.numpy as jnp
from jax import lax
import numpy as np

D_MODEL = 2048
BATCH = 8
SEQ = 2048
DEPTH = 2

CTX_LEN = 256
GRID_W = 64
HEAD_DIM = 128
N_HEADS = D_MODEL // HEAD_DIM
BRANCH_W = N_HEADS * HEAD_DIM
MLA_HEADS = N_HEADS
MLA_NOPE = 128
MLA_ROPE = 64
MLA_V = 128
KV_RANK = D_MODEL // 4
NA_HEADS = N_HEADS
NA_DIM = HEAD_DIM
NA_KR_MAX = 8
NA_KC = 16
GQA_HEADS = N_HEADS
GQA_KV_HEADS = N_HEADS // 4
GQA_GROUP = GQA_HEADS // GQA_KV_HEADS
GQA_DIM = HEAD_DIM
D_FF = 256 * ((8 * D_MODEL // 3 + 255) // 256)
CONV_W = 3
Q_BLOCK = 128
ROPE_THETA = 10000.0
ADA_EPS = 1e-6
POST_EPS = 1e-5
RMS_EPS = 1e-6
NEG_INF = -1e30
ALPHA = (2 * DEPTH) ** 0.25
BETA = (8 * DEPTH) ** -0.25
MLA_SCALE = (MLA_NOPE + MLA_ROPE) ** -0.5
NA_SCALE = NA_DIM ** -0.5
GQA_SCALE = GQA_DIM ** -0.5
SPLITS = (MLA_HEADS * (MLA_NOPE + MLA_ROPE),
          KV_RANK,
          MLA_ROPE,
          3 * NA_HEADS * NA_DIM,
          GQA_HEADS * GQA_DIM,
          2 * GQA_KV_HEADS * GQA_DIM,
          3 * D_MODEL)
SPLIT_IDX = tuple(int(v) for v in np.cumsum(SPLITS)[:-1])
N_IN = int(sum(SPLITS))

kernel_name = "hybrid_mla_na_gqa_convffn_deepnorm_dit"


def _layernorm(x, eps, g=None, b=None):
    xf = x.astype(jnp.float32)
    mu = jnp.mean(xf, -1, keepdims=True)
    var = jnp.mean(jnp.square(xf - mu), -1, keepdims=True)
    y = (xf - mu) * lax.rsqrt(var + eps)
    if g is not None:
        y = y * g.astype(jnp.float32) + b.astype(jnp.float32)
    return y.astype(x.dtype)


def _rmsnorm(x, g):
    xf = x.astype(jnp.float32)
    y = xf * lax.rsqrt(jnp.mean(xf * xf, -1, keepdims=True) + RMS_EPS) * g.astype(jnp.float32)
    return y.astype(x.dtype)


def _rope_1d(x, pos):
    half = x.shape[-1] // 2
    freqs = ROPE_THETA ** (-jnp.arange(half, dtype=jnp.float32) / half)
    ang = pos.astype(jnp.float32)[:, None] * freqs[None, :]
    cos = jnp.cos(ang)[None, :, None, :]
    sin = jnp.sin(ang)[None, :, None, :]
    xf = x.astype(jnp.float32)
    x1, x2 = xf[..., :half], xf[..., half:]
    return jnp.concatenate([x1 * cos - x2 * sin, x1 * sin + x2 * cos], -1).astype(x.dtype)


def _rope_2d(x, rows, cols):
    half = x.shape[-1] // 2
    return jnp.concatenate([_rope_1d(x[..., :half], rows), _rope_1d(x[..., half:], cols)], -1)


def _sdpa(q, k, v, scale):
    s = jnp.einsum('bqhgd,bkhd->bhgqk', q, k).astype(jnp.float32) * scale
    p = jax.nn.softmax(s, axis=-1).astype(v.dtype)
    return jnp.einsum('bhgqk,bkhe->bqhge', p, v)


def _blocked_sdpa(q, k, v, scale):
    B, S = q.shape[:2]
    nb = S // Q_BLOCK
    qb = jnp.moveaxis(q.reshape(B, nb, Q_BLOCK, *q.shape[2:]), 1, 0)
    out = lax.map(lambda qi: _sdpa(qi, k, v, scale), qb)
    return jnp.moveaxis(out, 0, 1).reshape(B, S, *out.shape[3:])


def _mla_query(mq, rows, cols):
    B, T = mq.shape[:2]
    q = mq.reshape(B, T, MLA_HEADS, MLA_NOPE + MLA_ROPE)
    if rows is not None:
        q = jnp.concatenate([q[..., :MLA_NOPE], _rope_2d(q[..., MLA_NOPE:], rows, cols)], -1)
    return q[:, :, :, None, :]


def _mla_kv(ckv, kr, kv_norm, w_ukv, rows, cols):
    B, T = ckv.shape[:2]
    c = _rmsnorm(ckv, kv_norm)
    kv = jnp.einsum('btr,rf->btf', c, w_ukv).reshape(B, T, MLA_HEADS, MLA_NOPE + MLA_V)
    k_nope, v = kv[..., :MLA_NOPE], kv[..., MLA_NOPE:]
    k_rope = kr[:, :, None, :]
    if rows is not None:
        k_rope = _rope_2d(k_rope, rows, cols)
    k = jnp.concatenate([k_nope, jnp.broadcast_to(k_rope, (B, T, MLA_HEADS, MLA_ROPE))], -1)
    return k, v


def _gqa_qkv(gq, gkv, q_norm, k_norm, rows, cols):
    B, T = gq.shape[:2]
    q = _rmsnorm(gq.reshape(B, T, GQA_HEADS, GQA_DIM), q_norm)
    kv = gkv.reshape(B, T, 2, GQA_KV_HEADS, GQA_DIM)
    k = _rmsnorm(kv[:, :, 0], k_norm)
    v = kv[:, :, 1]
    if rows is not None:
        q = _rope_2d(q, rows, cols)
        k = _rope_2d(k, rows, cols)
    return q.reshape(B, T, GQA_KV_HEADS, GQA_GROUP, GQA_DIM), k, v


def _na_latent(q, k, v, k_ctx, v_ctx, rpb):
    B, S, H, d = q.shape
    rows_n = S // GRID_W
    kr = min(NA_KR_MAX, rows_n)
    qg = q.reshape(B, rows_n, GRID_W, H, d)
    kg = k.reshape(B, rows_n, GRID_W, H, d)
    vg = v.reshape(B, rows_n, GRID_W, H, d)
    col = jnp.arange(GRID_W)
    cs = jnp.clip(col - NA_KC // 2, 0, GRID_W - NA_KC)
    col_valid = (col[None, :] >= cs[:, None]) & (col[None, :] < cs[:, None] + NA_KC)
    mask = jnp.broadcast_to(col_valid[:, None, :], (GRID_W, kr, GRID_W)).reshape(GRID_W, kr * GRID_W)
    dc_idx = jnp.clip(col[None, :] - col[:, None] + NA_KC - 1, 0, 2 * NA_KC - 2)
    n_lat = kr * GRID_W

    def row(r):
        rs = jnp.clip(r - kr // 2, 0, rows_n - kr)
        q_r = lax.dynamic_index_in_dim(qg, r, axis=1, keepdims=False)
        k_r = lax.dynamic_slice_in_dim(kg, rs, kr, axis=1).reshape(B, n_lat, H, d)
        v_r = lax.dynamic_slice_in_dim(vg, rs, kr, axis=1).reshape(B, n_lat, H, d)
        dr_idx = rs + jnp.arange(kr) - r + NA_KR_MAX - 1
        bias = jnp.take(rpb[:, dr_idx], dc_idx, axis=2)
        bias = jnp.transpose(bias, (0, 2, 1, 3)).reshape(H, GRID_W, n_lat).astype(jnp.float32)
        s_lat = jnp.einsum('bqhd,bkhd->bhqk', q_r, k_r).astype(jnp.float32) * NA_SCALE + bias
        s_lat = jnp.where(mask, s_lat, NEG_INF)
        s_ctx = jnp.einsum('bqhd,bkhd->bhqk', q_r, k_ctx).astype(jnp.float32) * NA_SCALE
        p = jax.nn.softmax(jnp.concatenate([s_lat, s_ctx], -1), axis=-1).astype(v.dtype)
        return (jnp.einsum('bhqk,bkhd->bqhd', p[..., :n_lat], v_r)
                + jnp.einsum('bhqk,bkhd->bqhd', p[..., n_lat:], v_ctx))

    out = lax.map(row, jnp.arange(rows_n))
    return jnp.moveaxis(out, 0, 1).reshape(B, S, H, d)


def _merge(ys, gates, w_branch, w_out):
    gs = jnp.split(gates, 3, axis=-1)
    acc = sum(jax.nn.sigmoid(g) * (y @ w_branch[i]) for i, (y, g) in enumerate(zip(ys, gs)))
    return acc @ w_out


def _token_mixers(h_lat, h_ctx, lp, rows, cols, with_ctx):
    B, S, _ = h_lat.shape
    L = h_ctx.shape[1]
    p_lat = h_lat @ lp['w_in']
    p_ctx = h_ctx @ lp['w_in']
    mq_l, ckv_l, kr_l, na_l, gq_l, gkv_l, gate_l = jnp.split(p_lat, SPLIT_IDX, axis=-1)
    mq_c, ckv_c, kr_c, na_c, gq_c, gkv_c, gate_c = jnp.split(p_ctx, SPLIT_IDX, axis=-1)

    qa_l = _mla_query(mq_l, rows, cols)
    ka_l, va_l = _mla_kv(ckv_l, kr_l, lp['mla_kv_norm'], lp['w_mla_ukv'], rows, cols)
    ka_c, va_c = _mla_kv(ckv_c, kr_c, lp['mla_kv_norm'], lp['w_mla_ukv'], None, None)
    ya_l = _blocked_sdpa(qa_l, jnp.concatenate([ka_l, ka_c], 1), jnp.concatenate([va_l, va_c], 1),
                         MLA_SCALE).reshape(B, S, BRANCH_W)

    qkv_l = na_l.reshape(B, S, 3, NA_HEADS, NA_DIM)
    qkv_c = na_c.reshape(B, L, 3, NA_HEADS, NA_DIM)
    yb_l = _na_latent(qkv_l[:, :, 0], qkv_l[:, :, 1], qkv_l[:, :, 2],
                      qkv_c[:, :, 1], qkv_c[:, :, 2], lp['na_rpb']).reshape(B, S, BRANCH_W)

    qc_l, kc_l, vc_l = _gqa_qkv(gq_l, gkv_l, lp['gqa_q_norm'], lp['gqa_k_norm'], rows, cols)
    qc_c, kc_c, vc_c = _gqa_qkv(gq_c, gkv_c, lp['gqa_q_norm'], lp['gqa_k_norm'], None, None)
    yc_l = _blocked_sdpa(qc_l, jnp.concatenate([kc_l, kc_c], 1), jnp.concatenate([vc_l, vc_c], 1),
                         GQA_SCALE).reshape(B, S, BRANCH_W)

    y_lat = _merge((ya_l, yb_l, yc_l), gate_l, lp['w_branch'], lp['w_out'])
    if not with_ctx:
        return y_lat, None
    ya_c = _sdpa(_mla_query(mq_c, None, None), ka_c, va_c, MLA_SCALE).reshape(B, L, BRANCH_W)
    yb_c = _sdpa(qkv_c[:, :, 0][:, :, :, None, :], qkv_c[:, :, 1], qkv_c[:, :, 2],
                 NA_SCALE).reshape(B, L, BRANCH_W)
    yc_c = _sdpa(qc_c, kc_c, vc_c, GQA_SCALE).reshape(B, L, BRANCH_W)
    y_ctx = _merge((ya_c, yb_c, yc_c), gate_c, lp['w_branch'], lp['w_out'])
    return y_lat, y_ctx


def _conv_ffn(h, w_up, conv_w, conv_b, w_down):
    u = h @ w_up
    gate, val = u[..., :D_FF], u[..., D_FF:]
    gp = jnp.pad(gate, ((0, 0), (1, 1), (0, 0)))
    gate = gp[:, :-2] * conv_w[0] + gp[:, 1:-1] * conv_w[1] + gp[:, 2:] * conv_w[2] + conv_b
    return (jax.nn.silu(gate) * val) @ w_down


def _modulate(x, shift, scale):
    return _layernorm(x, ADA_EPS) * (1 + scale) + shift


def setup_inputs(seed: int = 0) -> dict:
    key = jax.random.key(seed)
    ks = jax.random.split(key, 24)
    f32 = jnp.float32

    def nrm(k, shape, scale):
        return jax.random.normal(k, shape, f32) * scale

    L = DEPTH
    return {
        'x': nrm(ks[0], (BATCH, SEQ, D_MODEL), 1.0),
        'c': nrm(ks[1], (BATCH, D_MODEL), 1.0),
        'ctx': nrm(ks[2], (BATCH, CTX_LEN, D_MODEL), 1.0),
        'c_ctx': nrm(ks[3], (D_MODEL,), 1.0),
        'w_ada': nrm(ks[4], (L, D_MODEL, 6 * D_MODEL), 0.5 * D_MODEL ** -0.5),
        'b_ada': nrm(ks[5], (L, 6 * D_MODEL), 0.02),
        'w_in': nrm(ks[6], (L, D_MODEL, N_IN), D_MODEL ** -0.5),
        'mla_kv_norm': 1.0 + nrm(ks[7], (L, KV_RANK), 0.1),
        'w_mla_ukv': nrm(ks[8], (L, KV_RANK, MLA_HEADS * (MLA_NOPE + MLA_V)), KV_RANK ** -0.5),
        'gqa_q_norm': 1.0 + nrm(ks[9], (L, GQA_DIM), 0.1),
        'gqa_k_norm': 1.0 + nrm(ks[10], (L, GQA_DIM), 0.1),
        'na_rpb': nrm(ks[11], (L, NA_HEADS, 2 * NA_KR_MAX - 1, 2 * NA_KC - 1), 0.05),
        'w_branch': nrm(ks[12], (L, 3, BRANCH_W, D_MODEL), BETA * BRANCH_W ** -0.5),
        'w_out': nrm(ks[13], (L, D_MODEL, D_MODEL), BETA * D_MODEL ** -0.5),
        'ln_a_g': 1.0 + nrm(ks[14], (L, D_MODEL), 0.1),
        'ln_a_b': nrm(ks[15], (L, D_MODEL), 0.02),
        'w_up': nrm(ks[16], (L, D_MODEL, 2 * D_FF), D_MODEL ** -0.5),
        'conv_w': nrm(ks[17], (L, CONV_W, D_FF), CONV_W ** -0.5),
        'conv_b': nrm(ks[18], (L, D_FF), 0.02),
        'w_down': nrm(ks[19], (L, D_FF, D_MODEL), BETA * D_FF ** -0.5),
        'ln_f_g': 1.0 + nrm(ks[20], (L, D_MODEL), 0.1),
        'ln_f_b': nrm(ks[21], (L, D_MODEL), 0.02),
    }


def reference(x, c, ctx, c_ctx, w_ada, b_ada, w_in, mla_kv_norm, w_mla_ukv, gqa_q_norm, gqa_k_norm,
              na_rpb, w_branch, w_out, ln_a_g, ln_a_b, w_up, conv_w, conv_b, w_down, ln_f_g, ln_f_b):
    S = x.shape[1]
    t = jnp.arange(S)
    rows, cols = t // GRID_W, t % GRID_W
    x_lat, x_ctx = x, ctx
    for l in range(DEPTH):
        with_ctx = l < DEPTH - 1
        lp = {'w_in': w_in[l], 'mla_kv_norm': mla_kv_norm[l], 'w_mla_ukv': w_mla_ukv[l],
              'gqa_q_norm': gqa_q_norm[l], 'gqa_k_norm': gqa_k_norm[l], 'na_rpb': na_rpb[l],
              'w_branch': w_branch[l], 'w_out': w_out[l]}
        m_lat = (jax.nn.silu(c) @ w_ada[l] + b_ada[l]).reshape(c.shape[0], 1, 6, D_MODEL)
        m_ctx = (jax.nn.silu(c_ctx) @ w_ada[l] + b_ada[l]).reshape(1, 1, 6, D_MODEL)
        h_lat = _modulate(x_lat, m_lat[..., 0, :], m_lat[..., 1, :])
        h_ctx = _modulate(x_ctx, m_ctx[..., 0, :], m_ctx[..., 1, :])
        y_lat, y_ctx = _token_mixers(h_lat, h_ctx, lp, rows, cols, with_ctx)
        x_lat = _layernorm(ALPHA * x_lat + m_lat[..., 2, :] * y_lat, POST_EPS, ln_a_g[l], ln_a_b[l])
        h_lat = _modulate(x_lat, m_lat[..., 3, :], m_lat[..., 4, :])
        f_lat = _conv_ffn(h_lat, w_up[l], conv_w[l], conv_b[l], w_down[l])
        x_lat = _layernorm(ALPHA * x_lat + m_lat[..., 5, :] * f_lat, POST_EPS, ln_f_g[l], ln_f_b[l])
        if with_ctx:
            x_ctx = _layernorm(ALPHA * x_ctx + m_ctx[..., 2, :] * y_ctx, POST_EPS, ln_a_g[l], ln_a_b[l])
            h_ctx = _modulate(x_ctx, m_ctx[..., 3, :], m_ctx[..., 4, :])
            f_ctx = _conv_ffn(h_ctx, w_up[l], conv_w[l], conv_b[l], w_down[l])
            x_ctx = _layernorm(ALPHA * x_ctx + m_ctx[..., 5, :] * f_ctx, POST_EPS, ln_f_g[l], ln_f_b[l])
    return x_lat
```

```python
import functools

import jax
import jax.numpy as jnp
from jax import lax
from jax.experimental import pallas as pl
from jax.experimental.pallas import tpu as pltpu

F32 = jnp.float32
BF16 = jnp.bfloat16

GRID_W = 64
HEAD_DIM = 128
MLA_NOPE = 128
MLA_ROPE = 64
NA_KR = 8
NA_KC = 16
NA_QROWS = 4
NA_WROWS = NA_QROWS + NA_KR
ROPE_THETA = 10000.0
ADA_EPS = 1e-6
POST_EPS = 1e-5
RMS_EPS = 1e-6
NEG_INF = -1e30
LANES = 128
SUBLANES = 8
VMEM_LIMIT = 56 * 2**20


def _cp(*sem):
    return pltpu.CompilerParams(dimension_semantics=sem, vmem_limit_bytes=VMEM_LIMIT)


def _tile(cands, *ns):
    for c in cands:
        if all(n % c == 0 for n in ns):
            return c
    raise ValueError(f"no tile in {cands} divides {ns}")


def _ln(x, eps):
    mu = jnp.mean(x, -1, keepdims=True)
    xc = x - mu
    var = jnp.mean(xc * xc, -1, keepdims=True)
    return xc * lax.rsqrt(var + eps)


def _dot(a, b):
    return jnp.dot(a, b, preferred_element_type=F32)


def _dot_nt(a, b):
    return lax.dot_general(a, b, (((1,), (1,)), ((), ())), preferred_element_type=F32)


def _ada_kernel(c_ref, w_ref, b_ref, o_ref):
    c = c_ref[...]
    a = (c * jax.nn.sigmoid(c)).astype(BF16)
    o_ref[...] = _dot(a, w_ref[...].astype(BF16)) + b_ref[...]


def _ada(c_all, w_ada, b_ada):
    depth, d, n = w_ada.shape
    r = c_all.shape[0]
    tn = _tile((1024, 512, 256, 128), n)
    return pl.pallas_call(
        _ada_kernel,
        grid=(depth, n // tn),
        in_specs=[pl.BlockSpec((r, d), lambda l, j: (0, 0)),
                  pl.BlockSpec((None, d, tn), lambda l, j: (l, 0, j)),
                  pl.BlockSpec((None, 1, tn), lambda l, j: (l, 0, j))],
        out_specs=pl.BlockSpec((None, r, tn), lambda l, j: (l, 0, j)),
        out_shape=jax.ShapeDtypeStruct((depth, r, n), F32),
        compiler_params=_cp("parallel", "parallel"),
        name="ada",
    )(c_all, w_ada, b_ada.reshape(depth, 1, n))


def _ln_mod_kernel(x_ref, m_ref, h_ref):
    m = m_ref[...]
    h_ref[...] = (_ln(x_ref[...], ADA_EPS) * (1.0 + m[1:2]) + m[0:1]).astype(BF16)


def _ln_mod(x, mods, seq, nb):
    t, d = x.shape
    tm = _tile((512, 256, 128), seq, t)
    return pl.pallas_call(
        _ln_mod_kernel,
        grid=(t // tm,),
        in_specs=[pl.BlockSpec((tm, d), lambda i: (i, 0)),
                  pl.BlockSpec((None, 6, d), lambda i: (jnp.minimum(i * tm // seq, nb), 0, 0))],
        out_specs=pl.BlockSpec((tm, d), lambda i: (i, 0)),
        out_shape=jax.ShapeDtypeStruct((t, d), BF16),
        compiler_params=_cp("parallel"),
        name="ln_mod",
    )(x, mods)


def _mm_kernel(a_ref, w_ref, o_ref):
    o_ref[...] = _dot(a_ref[...], w_ref[...]).astype(o_ref.dtype)


def _mm_scale_kernel(a_ref, w_ref, cs_ref, o_ref):
    o_ref[...] = (_dot(a_ref[...], w_ref[...]) * cs_ref[...]).astype(o_ref.dtype)


def _mm_rms_kernel(a_ref, w_ref, g_ref, o_ref):
    acc = _dot(a_ref[...], w_ref[...])
    ms = jnp.mean(acc * acc, -1, keepdims=True)
    o_ref[...] = (acc * lax.rsqrt(ms + RMS_EPS) * g_ref[...]).astype(o_ref.dtype)


def _mm_rope_kernel(a_ref, w_ref, cs_ref, cos_ref, sa_ref, sb_ref, o_ref, *, shift, rms):
    acc = _dot(a_ref[...], w_ref[...])
    cos, sa, sb = cos_ref[...], sa_ref[...], sb_ref[...]
    for g in range(acc.shape[1] // LANES):
        sl = slice(g * LANES, (g + 1) * LANES)
        x = acc[:, sl]
        if rms:
            x = x * lax.rsqrt(jnp.mean(x * x, -1, keepdims=True) + RMS_EPS)
        x = x * cs_ref[:, sl]
        y = x * cos + pltpu.roll(x, LANES - shift, 1) * sa + pltpu.roll(x, shift, 1) * sb
        o_ref[:, sl] = y.astype(o_ref.dtype)


def _mm(a, w, *, rows=None, tm=None, tn=None, cs=None, rms_gain=None, rope=None, name="mm"):
    t, k = a.shape
    n = w.shape[1]
    rows = t if rows is None else rows
    tm = tm or _tile((1024, 512, 256, 128), rows)
    tn = tn or _tile((512, 256, 128), n)
    grid = (rows // tm, n // tn)
    a_spec = pl.BlockSpec((tm, k), lambda i, j: (i, 0))
    w_spec = pl.BlockSpec((k, tn), lambda i, j: (0, j))
    v_spec = pl.BlockSpec((1, tn), lambda i, j: (0, j))
    o_spec = pl.BlockSpec((tm, tn), lambda i, j: (i, j))
    if rope is not None:
        (cos, sa, sb), shift, rms, seq, nlat = rope
        nblk = seq // tm
        t_spec = pl.BlockSpec((tm, LANES), lambda i, j: (jnp.where(i * tm < nlat, i % nblk, nblk), 0))
        kern = functools.partial(_mm_rope_kernel, shift=shift, rms=rms)
        ins, specs = (a, w, cs, cos, sa, sb), [a_spec, w_spec, v_spec, t_spec, t_spec, t_spec]
    elif rms_gain is not None:
        kern, ins, specs = _mm_rms_kernel, (a, w, rms_gain), [a_spec, w_spec, v_spec]
    elif cs is not None:
        kern, ins, specs = _mm_scale_kernel, (a, w, cs), [a_spec, w_spec, v_spec]
    else:
        kern, ins, specs = _mm_kernel, (a, w), [a_spec, w_spec]
    return pl.pallas_call(
        kern, grid=grid, in_specs=specs, out_specs=o_spec,
        out_shape=jax.ShapeDtypeStruct((rows, n), BF16),
        compiler_params=_cp("parallel", "parallel"), name=name,
    )(*ins)


def _softmax_pv(parts):
    m = None
    for s, _ in parts:
        mi = jnp.max(s, -1, keepdims=True)
        m = mi if m is None else jnp.maximum(m, mi)
    den, out = None, None
    for s, v in parts:
        e = jnp.exp(s - m)
        di = jnp.sum(e, -1, keepdims=True)
        oi = _dot(e.astype(BF16), v)
        den = di if den is None else den + di
        out = oi if out is None else out + oi
    return out * (1.0 / den)


def _gattn_kernel(*refs, group, with_lat):
    if with_lat:
        q_ref, kl_ref, vl_ref, kc_ref, vc_ref, o_ref = refs
    else:
        q_ref, kc_ref, vc_ref, o_ref = refs
    for g in range(group):
        sl = slice(g * HEAD_DIM, (g + 1) * HEAD_DIM)
        q = q_ref[:, sl]
        parts = []
        if with_lat:
            parts.append((_dot_nt(q, kl_ref[...]), vl_ref[...]))
        parts.append((_dot_nt(q, kc_ref[...]), vc_ref[...]))
        o_ref[:, sl] = _softmax_pv(parts).astype(o_ref.dtype)


def _gattn(q_arr, q_col, k_arr, k_col, v_arr, v_col, *, group, n_kv, nb, seq, ctx, with_lat, out_rows, name):
    nlat_c = nb * seq // ctx
    if with_lat:
        tq = _tile((512, 256, 128), seq)
        nq = seq // tq
        q_map = lambda b, h, i: (b * nq + i, q_col + h)
    else:
        tq, nq = ctx, 1
        q_map = lambda b, h, i: (nlat_c + b, q_col + h)
    gw = group * HEAD_DIM
    specs = [pl.BlockSpec((tq, gw), q_map)]
    ins = [q_arr]
    if with_lat:
        specs += [pl.BlockSpec((seq, HEAD_DIM), lambda b, h, i: (b, k_col + h)),
                  pl.BlockSpec((seq, HEAD_DIM), lambda b, h, i: (b, v_col + h))]
        ins += [k_arr, v_arr]
    specs += [pl.BlockSpec((ctx, HEAD_DIM), lambda b, h, i: (nlat_c + b, k_col + h)),
              pl.BlockSpec((ctx, HEAD_DIM), lambda b, h, i: (nlat_c + b, v_col + h))]
    ins += [k_arr, v_arr]
    o_map = (lambda b, h, i: (b * nq + i, h)) if with_lat else (lambda b, h, i: (b, h))
    return pl.pallas_call(
        functools.partial(_gattn_kernel, group=group, with_lat=with_lat),
        grid=(nb, n_kv, nq), in_specs=specs,
        out_specs=pl.BlockSpec((tq, gw), o_map),
        out_shape=jax.ShapeDtypeStruct((out_rows, n_kv * gw), BF16),
        compiler_params=_cp("parallel", "parallel", "arbitrary"), name=name,
    )(*ins)


def _mla_kernel(*refs, with_lat):
    if with_lat:
        qn_ref, qr_ref, kvl_ref, krl_ref, kvc_ref, krc_ref, o_ref = refs
    else:
        qn_ref, qr_ref, kvc_ref, krc_ref, o_ref = refs
    for hh in range(2):
        qn = qn_ref[:, hh * MLA_NOPE:(hh + 1) * MLA_NOPE]
        qr = qr_ref[:, hh * MLA_ROPE:(hh + 1) * MLA_ROPE]
        c0 = hh * 2 * HEAD_DIM
        parts = []
        srcs = ([(kvl_ref, krl_ref)] if with_lat else []) + [(kvc_ref, krc_ref)]
        for kv_ref, kr_ref in srcs:
            s = _dot_nt(qn, kv_ref[:, c0:c0 + MLA_NOPE]) + _dot_nt(qr, kr_ref[:, :MLA_ROPE])
            parts.append((s, kv_ref[:, c0 + MLA_NOPE:c0 + 2 * HEAD_DIM]))
        o_ref[:, hh * HEAD_DIM:(hh + 1) * HEAD_DIM] = _softmax_pv(parts).astype(o_ref.dtype)


def _mla(qn_arr, qn_col, qr_arr, kv_arr, kr_arr, kr_col, *, heads, nb, seq, ctx, with_lat, out_rows, name):
    nlat_c = nb * seq // ctx
    hp = heads // 2
    if with_lat:
        tq = _tile((512, 256, 128), seq)
        nq = seq // tq
        row = lambda b, i: b * nq + i
    else:
        tq, nq = ctx, 1
        row = lambda b, i: nlat_c + b
    specs = [pl.BlockSpec((tq, 2 * MLA_NOPE), lambda b, h, i: (row(b, i), qn_col + h)),
             pl.BlockSpec((tq, 2 * MLA_ROPE), lambda b, h, i: (row(b, i), h))]
    ins = [qn_arr, qr_arr]
    if with_lat:
        specs += [pl.BlockSpec((seq, 4 * HEAD_DIM), lambda b, h, i: (b, h)),
                  pl.BlockSpec((seq, LANES), lambda b, h, i: (b, kr_col))]
        ins += [kv_arr, kr_arr]
    specs += [pl.BlockSpec((ctx, 4 * HEAD_DIM), lambda b, h, i: (nlat_c + b, h)),
              pl.BlockSpec((ctx, LANES), lambda b, h, i: (nlat_c + b, kr_col))]
    ins += [kv_arr, kr_arr]
    o_map = (lambda b, h, i: (b * nq + i, h)) if with_lat else (lambda b, h, i: (b, h))
    return pl.pallas_call(
        functools.partial(_mla_kernel, with_lat=with_lat),
        grid=(nb, hp, nq), in_specs=specs,
        out_specs=pl.BlockSpec((tq, 2 * HEAD_DIM), o_map),
        out_shape=jax.ShapeDtypeStruct((out_rows, heads * HEAD_DIM), BF16),
        compiler_params=_cp("parallel", "parallel", "arbitrary"), name=name,
    )(*ins)


def _na_kernel(q_ref, k_ref, v_ref, kc_ref, vc_ref, bias_ref, o_ref, *, nblk):
    qt = NA_QROWS * GRID_W
    wt = NA_WROWS * GRID_W
    kc, vc = kc_ref[...], vc_ref[...]

    def body(rb, carry):
        q0 = pl.multiple_of(rb * qt, qt)
        w0 = pl.multiple_of(jnp.clip(rb - 1, 0, nblk - 3) * qt, qt)
        pat = jnp.where(rb == 0, 0, jnp.where(rb == nblk - 1, 2, 1))
        q = q_ref[pl.ds(q0, qt), :]
        s_lat = _dot_nt(q, k_ref[pl.ds(w0, wt), :]) + bias_ref[pat]
        s_ctx = _dot_nt(q, kc)
        o = _softmax_pv([(s_lat, v_ref[pl.ds(w0, wt), :]), (s_ctx, vc)])
        o_ref[pl.ds(q0, qt), :] = o.astype(o_ref.dtype)
        return carry

    lax.fori_loop(0, nblk, body, 0)


def _na(p_arr, bias, *, heads, nb, seq, ctx, out_rows):
    nlat_c = nb * seq // ctx
    nblk = seq // (NA_QROWS * GRID_W)
    qt, wt = NA_QROWS * GRID_W, NA_WROWS * GRID_W
    lat = lambda off: pl.BlockSpec((seq, HEAD_DIM), lambda h, b: (b, off + h))
    cx = lambda off: pl.BlockSpec((ctx, HEAD_DIM), lambda h, b: (nlat_c + b, off + h))
    return pl.pallas_call(
        functools.partial(_na_kernel, nblk=nblk),
        grid=(heads, nb),
        in_specs=[lat(0), lat(heads), lat(2 * heads), cx(heads), cx(2 * heads),
                  pl.BlockSpec((None, 3, qt, wt), lambda h, b: (h, 0, 0, 0))],
        out_specs=pl.BlockSpec((seq, HEAD_DIM), lambda h, b: (b, h)),
        out_shape=jax.ShapeDtypeStruct((out_rows, heads * HEAD_DIM), BF16),
        compiler_params=_cp("parallel", "arbitrary"), name="na_lat",
    )(p_arr, p_arr, p_arr, p_arr, p_arr, bias)


def _merge_kernel(ya_ref, yb_ref, yc_ref, w_ref, ga_ref, gb_ref, gc_ref, o_ref):
    acc = None
    for i, (y_ref, g_ref) in enumerate(((ya_ref, ga_ref), (yb_ref, gb_ref), (yc_ref, gc_ref))):
        t = jax.nn.sigmoid(g_ref[...].astype(F32)) * _dot(y_ref[...], w_ref[i])
        acc = t if acc is None else acc + t
    o_ref[...] = acc.astype(o_ref.dtype)


def _merge(ya, yb, yc, w_branch, p_arr, gate_col, *, rows):
    d = ya.shape[1]
    n = w_branch.shape[2]
    tm = _tile((512, 256, 128), rows)
    tn = _tile((512, 256, 128), n)
    y_spec = pl.BlockSpec((tm, d), lambda i, j: (i, 0))
    g_spec = lambda b: pl.BlockSpec((tm, tn), lambda i, j: (i, (gate_col + b * n) // tn + j))
    return pl.pallas_call(
        _merge_kernel, grid=(rows // tm, n // tn),
        in_specs=[y_spec, y_spec, y_spec, pl.BlockSpec((3, d, tn), lambda i, j: (0, 0, j)),
                  g_spec(0), g_spec(1), g_spec(2)],
        out_specs=pl.BlockSpec((tm, tn), lambda i, j: (i, j)),
        out_shape=jax.ShapeDtypeStruct((rows, n), BF16),
        compiler_params=_cp("parallel", "parallel"), name="merge",
    )(ya, yb, yc, w_branch, p_arr, p_arr, p_arr)


def _post_ln_epilogue(y, x, m, gate_row, g, b, alpha):
    z = alpha * x + m[gate_row:gate_row + 1] * y
    return _ln(z, POST_EPS) * g + b


def _out_ln_kernel(a_ref, w_ref, x_ref, m_ref, g_ref, b_ref, xo_ref, h_ref, *, alpha):
    m = m_ref[...]
    xn = _post_ln_epilogue(_dot(a_ref[...], w_ref[...]), x_ref[...], m, 2, g_ref[...], b_ref[...], alpha)
    xo_ref[...] = xn
    h_ref[...] = (_ln(xn, ADA_EPS) * (1.0 + m[4:5]) + m[3:4]).astype(BF16)


def _out_ln(a, w_out, x, mods, ln_g, ln_b, *, rows, seq, nb, alpha):
    d = x.shape[1]
    tm = _tile((256, 128), rows, seq)
    row = pl.BlockSpec((tm, d), lambda i: (i, 0))
    vec = pl.BlockSpec((1, d), lambda i: (0, 0))
    return pl.pallas_call(
        functools.partial(_out_ln_kernel, alpha=alpha), grid=(rows // tm,),
        in_specs=[row, pl.BlockSpec((d, d), lambda i: (0, 0)), row,
                  pl.BlockSpec((None, 6, d), lambda i: (jnp.minimum(i * tm // seq, nb), 0, 0)), vec, vec],
        out_specs=[row, row],
        out_shape=[jax.ShapeDtypeStruct((rows, d), F32), jax.ShapeDtypeStruct((rows, d), BF16)],
        compiler_params=_cp("parallel"), name="out_ln",
    )(a, w_out, x, mods, ln_g, ln_b)


def _conv_kernel(g_ref, v_ref, gp_ref, gn_ref, cw_ref, cb_ref, o_ref, *, tm, seq, ctx, nlat):
    g = g_ref[...].astype(F32)
    base = pl.program_id(0) * tm
    in_lat = base < nlat
    ridx = lax.broadcasted_iota(jnp.int32, (tm, 1), 0)
    pos = jnp.where(in_lat, base % seq, (base - nlat) % ctx) + ridx
    last = jnp.where(in_lat, seq - 1, ctx - 1)
    prev_row = gp_ref[SUBLANES - 1:SUBLANES, :].astype(F32)
    next_row = gn_ref[0:1, :].astype(F32)
    g_prev = jnp.where(ridx == 0, prev_row, pltpu.roll(g, 1, 0))
    g_next = jnp.where(ridx == tm - 1, next_row, pltpu.roll(g, tm - 1, 0))
    g_prev = jnp.where(pos == 0, 0.0, g_prev)
    g_next = jnp.where(pos == last, 0.0, g_next)
    cw = cw_ref[...]
    gc = g_prev * cw[0:1] + g * cw[1:2] + g_next * cw[2:3] + cb_ref[...]
    o_ref[...] = (gc * jax.nn.sigmoid(gc) * v_ref[...].astype(F32)).astype(o_ref.dtype)


def _conv_gate(u, conv_w, conv_b, *, rows, seq, ctx, nlat):
    dff = conv_w.shape[1]
    tm = _tile((512, 256, 128), rows, ctx)
    tf = _tile((512, 256, 128), dff)
    nf = dff // tf
    hb = tm // SUBLANES
    last_hb = rows // SUBLANES - 1
    return pl.pallas_call(
        functools.partial(_conv_kernel, tm=tm, seq=seq, ctx=ctx, nlat=nlat),
        grid=(rows // tm, nf),
        in_specs=[pl.BlockSpec((tm, tf), lambda i, j: (i, j)),
                  pl.BlockSpec((tm, tf), lambda i, j: (i, nf + j)),
                  pl.BlockSpec((SUBLANES, tf), lambda i, j: (jnp.maximum(i * hb - 1, 0), j)),
                  pl.BlockSpec((SUBLANES, tf), lambda i, j: (jnp.minimum((i + 1) * hb, last_hb), j)),
                  pl.BlockSpec((3, tf), lambda i, j: (0, j)),
                  pl.BlockSpec((1, tf), lambda i, j: (0, j))],
        out_specs=pl.BlockSpec((tm, tf), lambda i, j: (i, j)),
        out_shape=jax.ShapeDtypeStruct((rows, dff), BF16),
        compiler_params=_cp("parallel", "parallel"), name="conv_gate",
    )(u, u, u, u, conv_w, conv_b)


def _down_ln_kernel(*refs, alpha, nk, with_next):
    if with_next:
        a_ref, w_ref, x_ref, m_ref, g_ref, b_ref, mn_ref, xo_ref, h_ref, acc_ref = refs
    else:
        a_ref, w_ref, x_ref, m_ref, g_ref, b_ref, xo_ref, acc_ref = refs
    k = pl.program_id(1)

    @pl.when(k == 0)
    def _():
        acc_ref[...] = jnp.zeros_like(acc_ref)

    acc_ref[...] += _dot(a_ref[...], w_ref[...])

    @pl.when(k == nk - 1)
    def _():
        xn = _post_ln_epilogue(acc_ref[...], x_ref[...], m_ref[...], 5, g_ref[...], b_ref[...], alpha)
        xo_ref[...] = xn
        if with_next:
            mn = mn_ref[...]
            h_ref[...] = (_ln(xn, ADA_EPS) * (1.0 + mn[1:2]) + mn[0:1]).astype(BF16)


def _down_ln(a, w_down, x, mods, ln_g, ln_b, mods_next, *, rows, seq, nb, alpha):
    d = x.shape[1]
    dff = a.shape[1]
    tm = _tile((512, 256, 128), rows, seq)
    nk = next(c for c in (4, 2, 1, 8, 11, 22, 44) if dff % c == 0 and (dff // c) % LANES == 0)
    tk = dff // nk
    with_next = mods_next is not None
    row = pl.BlockSpec((tm, d), lambda i, k: (i, 0))
    vec = pl.BlockSpec((1, d), lambda i, k: (0, 0))
    mod = pl.BlockSpec((None, 6, d), lambda i, k: (jnp.minimum(i * tm // seq, nb), 0, 0))
    specs = [pl.BlockSpec((tm, tk), lambda i, k: (i, k)), pl.BlockSpec((tk, d), lambda i, k: (k, 0)),
             row, mod, vec, vec]
    ins = [a, w_down, x, mods, ln_g, ln_b]
    out_specs = [row]
    out_shape = [jax.ShapeDtypeStruct((rows, d), F32)]
    if with_next:
        specs.append(mod)
        ins.append(mods_next)
        out_specs.append(row)
        out_shape.append(jax.ShapeDtypeStruct((rows, d), BF16))
    res = pl.pallas_call(
        functools.partial(_down_ln_kernel, alpha=alpha, nk=nk, with_next=with_next),
        grid=(rows // tm, nk), in_specs=specs, out_specs=out_specs, out_shape=out_shape,
        scratch_shapes=[pltpu.VMEM((tm, d), F32)],
        compiler_params=_cp("parallel", "arbitrary"), name="down_ln",
    )(*ins)
    return (res[0], res[1]) if with_next else (res[0], None)


def _rope_tables(seq, tm, dims, half):
    t = jnp.arange(seq)
    rows, cols = (t // GRID_W).astype(F32), (t % GRID_W).astype(F32)
    j = jnp.arange(LANES)
    freqs = ROPE_THETA ** (-jnp.arange(half, dtype=F32) / half)
    f = freqs[j % half]
    use_row = (j % dims) < dims // 2
    ang = jnp.where(use_row[None, :], rows[:, None], cols[:, None]) * f[None, :]
    cos, sin = jnp.cos(ang), jnp.sin(ang)
    upper = (j % (2 * half)) < half
    sa = jnp.where(upper[None, :], -sin, 0.0)
    sb = jnp.where(upper[None, :], 0.0, sin)
    ident = jnp.zeros((tm, LANES), F32)
    return (jnp.concatenate([cos, ident + 1.0], 0), jnp.concatenate([sa, ident], 0),
            jnp.concatenate([sb, ident], 0))


def _na_bias(rpb):
    rq = jnp.arange(NA_QROWS)
    rk = jnp.arange(NA_WROWS)
    r0_rel = jnp.array([0, NA_KR // 2, NA_WROWS - NA_QROWS])[:, None]
    rs_rel = jnp.stack([jnp.zeros_like(rq), rq, jnp.full_like(rq, NA_WROWS - NA_KR)])
    row_ok = (rk[None, None, :] >= rs_rel[:, :, None]) & (rk[None, None, :] < rs_rel[:, :, None] + NA_KR)
    dr = jnp.clip(rk[None, None, :] - r0_rel[:, :, None] - rq[None, :, None] + NA_KR - 1, 0, 2 * NA_KR - 2)
    col = jnp.arange(GRID_W)
    cs = jnp.clip(col - NA_KC // 2, 0, GRID_W - NA_KC)
    col_ok = (col[None, :] >= cs[:, None]) & (col[None, :] < cs[:, None] + NA_KC)
    dc = jnp.clip(col[None, :] - col[:, None] + NA_KC - 1, 0, 2 * NA_KC - 2)
    vals = rpb[:, dr[:, :, None, :, None], dc[None, None, :, None, :]]
    ok = row_ok[:, :, None, :, None] & col_ok[None, None, :, None, :]
    out = jnp.where(ok[None], vals.astype(F32), NEG_INF)
    return out.reshape(rpb.shape[0], 3, NA_QROWS * GRID_W, NA_WROWS * GRID_W)


def kernel(x, c, ctx, c_ctx, w_ada, b_ada, w_in, mla_kv_norm, w_mla_ukv, gqa_q_norm, gqa_k_norm, na_rpb,
           w_branch, w_out, ln_a_g, ln_a_b, w_up, conv_w, conv_b, w_down, ln_f_g, ln_f_b):
    nb, seq, d = x.shape
    lctx = ctx.shape[1]
    depth = w_in.shape[0]
    heads = d // HEAD_DIM
    gkv = heads // 4
    group = heads // gkv
    kvr = mla_kv_norm.shape[1]
    dff = w_down.shape[1]
    nlat, nctx = nb * seq, nb * lctx
    t_all = nlat + nctx
    alpha = float((2 * depth) ** 0.25)
    mla_scale = float((MLA_NOPE + MLA_ROPE) ** -0.5)
    hd_scale = float(HEAD_DIM ** -0.5)
    assert seq % (NA_QROWS * GRID_W) == 0 and seq // (NA_QROWS * GRID_W) >= 3
    assert heads % 4 == 0 and seq % lctx == 0 and nctx <= seq

    o_ckv = heads * (MLA_NOPE + MLA_ROPE)
    o_kr = o_ckv + kvr
    o_na = o_kr + MLA_ROPE
    o_gq = o_na + 3 * heads * HEAD_DIM
    o_gkv = o_gq + heads * HEAD_DIM
    o_gate = o_gkv + 2 * gkv * HEAD_DIM

    pad = (-(nb + 1)) % SUBLANES
    c_all = jnp.concatenate([c, c_ctx[None, :], jnp.zeros((pad, d), F32)], 0)
    mods = _ada(c_all, w_ada, b_ada).reshape(depth, nb + 1 + pad, 6, d)

    tm_proj = _tile((1024, 512, 256, 128), seq, nctx)
    rope_g = _rope_tables(seq, tm_proj, HEAD_DIM, HEAD_DIM // 4)
    rope_m = _rope_tables(seq, tm_proj, MLA_ROPE, MLA_ROPE // 4)

    xs = jnp.concatenate([x.reshape(nlat, d), ctx.reshape(nctx, d)], 0)
    h = _ln_mod(xs, mods[0], seq, nb)

    c_naq, c_nak, c_nav = 0, heads, 2 * heads
    c_gate = 3 * heads
    c_mqn = c_gate + 3 * heads
    c_gv = c_mqn + heads

    for l in range(depth):
        last = l == depth - 1
        rows = nlat if last else t_all
        wl = w_in[l]
        mq = wl[:, :o_ckv].reshape(d, heads, MLA_NOPE + MLA_ROPE)
        gkvw = wl[:, o_gkv:o_gate].reshape(d, 2, gkv * HEAD_DIM)
        w_plain = jnp.concatenate([wl[:, o_na:o_gq], wl[:, o_gate:],
                                   mq[:, :, :MLA_NOPE].reshape(d, heads * MLA_NOPE), gkvw[:, 1]], 1).astype(BF16)
        cs_plain = jnp.concatenate([jnp.full((heads * HEAD_DIM,), hd_scale, F32),
                                    jnp.ones((5 * heads * HEAD_DIM,), F32),
                                    jnp.full((heads * MLA_NOPE,), mla_scale, F32),
                                    jnp.ones((gkv * HEAD_DIM,), F32)])[None, :]
        w_mrope = jnp.concatenate([mq[:, :, MLA_NOPE:].reshape(d, heads * MLA_ROPE), wl[:, o_kr:o_na],
                                   jnp.zeros((d, LANES - MLA_ROPE), F32)], 1).astype(BF16)
        cs_mrope = jnp.concatenate([jnp.full((heads * MLA_ROPE,), mla_scale, F32), jnp.ones((LANES,), F32)])[None, :]
        w_gqa = jnp.concatenate([wl[:, o_gq:o_gkv], gkvw[:, 0]], 1).astype(BF16)
        cs_gqa = jnp.concatenate([jnp.tile(gqa_q_norm[l] * hd_scale, heads), jnp.tile(gqa_k_norm[l], gkv)])[None, :]

        p_plain = _mm(h, w_plain, tm=tm_proj, cs=cs_plain, name="proj_plain")
        p_ckv = _mm(h, wl[:, o_ckv:o_kr].astype(BF16), tm=tm_proj, tn=kvr, rms_gain=mla_kv_norm[l][None, :],
                    name="proj_ckv")
        p_mrope = _mm(h, w_mrope, tm=tm_proj, cs=cs_mrope, rope=(rope_m, MLA_ROPE // 4, False, seq, nlat),
                      name="proj_mla_rope")
        p_gqa = _mm(h, w_gqa, tm=tm_proj, cs=cs_gqa, rope=(rope_g, HEAD_DIM // 4, True, seq, nlat),
                    name="proj_gqa")
        p_kv = _mm(p_ckv, w_mla_ukv[l].astype(BF16), tm=tm_proj, name="mla_ukv")

        dims = dict(nb=nb, seq=seq, ctx=lctx)
        ya = _mla(p_plain, c_mqn // 2, p_mrope, p_kv, p_mrope, heads * MLA_ROPE // LANES, heads=heads,
                  with_lat=True, out_rows=rows, name="mla_lat", **dims)
        yb = _na(p_plain, _na_bias(na_rpb[l]), heads=heads, out_rows=rows, **dims)
        yc = _gattn(p_gqa, 0, p_gqa, heads, p_plain, c_gv, group=group, n_kv=gkv, with_lat=True,
                    out_rows=rows, name="gqa_lat", **dims)
        if not last:
            ya_c = _mla(p_plain, c_mqn // 2, p_mrope, p_kv, p_mrope, heads * MLA_ROPE // LANES, heads=heads,
                        with_lat=False, out_rows=nctx, name="mla_ctx", **dims)
            yb_c = _gattn(p_plain, c_naq, p_plain, c_nak, p_plain, c_nav, group=1, n_kv=heads, with_lat=False,
                          out_rows=nctx, name="na_ctx", **dims)
            yc_c = _gattn(p_gqa, 0, p_gqa, heads, p_plain, c_gv, group=group, n_kv=gkv, with_lat=False,
                          out_rows=nctx, name="gqa_ctx", **dims)
            ya = lax.dynamic_update_slice(ya, ya_c, (nlat, 0))
            yb = lax.dynamic_update_slice(yb, yb_c, (nlat, 0))
            yc = lax.dynamic_update_slice(yc, yc_c, (nlat, 0))

        merged = _merge(ya, yb, yc, w_branch[l].astype(BF16), p_plain, c_gate * HEAD_DIM, rows=rows)
        xs, h2 = _out_ln(merged, w_out[l].astype(BF16), xs, mods[l], ln_a_g[l][None, :], ln_a_b[l][None, :],
                         rows=rows, seq=seq, nb=nb, alpha=alpha)
        u = _mm(h2, w_up[l].astype(BF16), rows=rows, name="ffn_up")
        hid = _conv_gate(u, conv_w[l], conv_b[l][None, :], rows=rows, seq=seq, ctx=lctx, nlat=nlat)
        xs, h = _down_ln(hid, w_down[l].astype(BF16), xs, mods[l], ln_f_g[l][None, :], ln_f_b[l][None, :],
                         None if last else mods[l + 1], rows=rows, seq=seq, nb=nb, alpha=alpha)
    return xs[:nlat].reshape(nb, seq, d)
```

```python
import functools

import jax
import jax.numpy as jnp
from jax import lax
from jax.experimental import pallas as pl
from jax.experimental.pallas import tpu as pltpu

F32 = jnp.float32
BF16 = jnp.bfloat16

GRID_W = 64
HEAD_DIM = 128
MLA_NOPE = 128
MLA_ROPE = 64
NA_KR = 8
NA_KC = 16
NA_QROWS = 4
NA_WROWS = NA_QROWS + NA_KR
ROPE_THETA = 10000.0
ADA_EPS = 1e-6
POST_EPS = 1e-5
RMS_EPS = 1e-6
NEG_INF = -1e30
LANES = 128
SUBLANES = 8
VMEM_LIMIT = 56 * 2**20


def _cp(*sem):
    return pltpu.CompilerParams(dimension_semantics=sem, vmem_limit_bytes=VMEM_LIMIT)


def _tile(cands, *ns):
    for c in cands:
        if all(n % c == 0 for n in ns):
            return c
    raise ValueError(f"no tile in {cands} divides {ns}")


def _ln(x, eps):
    mu = jnp.mean(x, -1, keepdims=True)
    xc = x - mu
    var = jnp.mean(xc * xc, -1, keepdims=True)
    return xc * lax.rsqrt(var + eps)


def _rms(x):
    return x * lax.rsqrt(jnp.mean(x * x, -1, keepdims=True) + RMS_EPS)


def _rope(x, tabs, shift):
    cos, sa, sb = tabs
    return x * cos + pltpu.roll(x, LANES - shift, 1) * sa + pltpu.roll(x, shift, 1) * sb


def _dot(a, b):
    return jnp.dot(a, b, preferred_element_type=F32)


def _dot_nt(a, b):
    return lax.dot_general(a, b, (((1,), (1,)), ((), ())), preferred_element_type=F32)


def _ada_kernel(c_ref, w_ref, b_ref, o_ref):
    c = c_ref[...]
    a = (c * jax.nn.sigmoid(c)).astype(BF16)
    o_ref[...] = _dot(a, w_ref[...].astype(BF16)) + b_ref[...]


def _ada(c_all, w_ada, b_ada):
    depth, d, n = w_ada.shape
    r = c_all.shape[0]
    tn = _tile((1024, 512, 256, 128), n)
    return pl.pallas_call(
        _ada_kernel,
        grid=(depth, n // tn),
        in_specs=[pl.BlockSpec((r, d), lambda l, j: (0, 0)),
                  pl.BlockSpec((None, d, tn), lambda l, j: (l, 0, j)),
                  pl.BlockSpec((None, 1, tn), lambda l, j: (l, 0, j))],
        out_specs=pl.BlockSpec((None, r, tn), lambda l, j: (l, 0, j)),
        out_shape=jax.ShapeDtypeStruct((depth, r, n), F32),
        compiler_params=_cp("parallel", "parallel"),
        name="ada",
    )(c_all, w_ada, b_ada.reshape(depth, 1, n))


def _ln_mod_kernel(x_ref, m_ref, h_ref):
    m = m_ref[...]
    h_ref[...] = (_ln(x_ref[...], ADA_EPS) * (1.0 + m[1:2]) + m[0:1]).astype(BF16)


def _ln_mod(x, mods, seq, nb):
    t, d = x.shape
    tm = _tile((512, 256, 128), seq, t)
    return pl.pallas_call(
        _ln_mod_kernel,
        grid=(t // tm,),
        in_specs=[pl.BlockSpec((tm, d), lambda i: (i, 0)),
                  pl.BlockSpec((None, 6, d), lambda i: (jnp.minimum(i * tm // seq, nb), 0, 0))],
        out_specs=pl.BlockSpec((tm, d), lambda i: (i, 0)),
        out_shape=jax.ShapeDtypeStruct((t, d), BF16),
        compiler_params=_cp("parallel"),
        name="ln_mod",
    )(x, mods)


def _mm_kernel(a_ref, w_ref, o_ref):
    o_ref[...] = _dot(a_ref[...], w_ref[...]).astype(o_ref.dtype)


def _mm_scale_kernel(a_ref, w_ref, cs_ref, o_ref):
    o_ref[...] = (_dot(a_ref[...], w_ref[...]) * cs_ref[...]).astype(o_ref.dtype)


def _mm_rms_kernel(a_ref, w_ref, g_ref, o_ref):
    o_ref[...] = (_rms(_dot(a_ref[...], w_ref[...])) * g_ref[...]).astype(o_ref.dtype)


def _mm(a, w, *, rows=None, tm=None, tn=None, cs=None, rms_gain=None, name="mm"):
    t, k = a.shape
    n = w.shape[1]
    rows = t if rows is None else rows
    tm = tm or _tile((1024, 512, 256, 128), rows)
    tn = tn or _tile((640, 512, 384, 256, 128), n)
    a_spec = pl.BlockSpec((tm, k), lambda i, j: (i, 0))
    w_spec = pl.BlockSpec((k, tn), lambda i, j: (0, j))
    v_spec = pl.BlockSpec((1, tn), lambda i, j: (0, j))
    if rms_gain is not None:
        kern, ins, specs = _mm_rms_kernel, (a, w, rms_gain), [a_spec, w_spec, v_spec]
    elif cs is not None:
        kern, ins, specs = _mm_scale_kernel, (a, w, cs), [a_spec, w_spec, v_spec]
    else:
        kern, ins, specs = _mm_kernel, (a, w), [a_spec, w_spec]
    return pl.pallas_call(
        kern, grid=(rows // tm, n // tn), in_specs=specs,
        out_specs=pl.BlockSpec((tm, tn), lambda i, j: (i, j)),
        out_shape=jax.ShapeDtypeStruct((rows, n), BF16),
        compiler_params=_cp("parallel", "parallel"), name=name,
    )(*ins)


def _softmax_pv(s, v):
    e = jnp.exp(s - jnp.max(s, -1, keepdims=True))
    den = jnp.sum(e, -1, keepdims=True)
    return _dot(e.astype(BF16), v) * (1.0 / den)


def _gattn_kernel(*refs, group, with_lat, rms):
    refs = list(refs)
    q_ref = refs.pop(0)
    kl_ref, vl_ref = (refs.pop(0), refs.pop(0)) if with_lat else (None, None)
    kc_ref, vc_ref = refs.pop(0), refs.pop(0)
    qg_ref, kg_ref = (refs.pop(0), refs.pop(0)) if rms else (None, None)
    if with_lat:
        qtabs = tuple(r[...] for r in refs[:3])
        ktab_refs = refs[3:6]
        o_ref, kf_ref, vf_ref = refs[6:]
    else:
        qtabs = None
        (o_ref,) = refs

    def prep(x, g_ref, tabs):
        if not rms and tabs is None:
            return x
        x = x.astype(F32)
        if rms:
            x = _rms(x) * g_ref[...]
        if tabs is not None:
            x = _rope(x, tabs, HEAD_DIM // 4)
        return x.astype(BF16)

    if with_lat:
        nl = kl_ref.shape[0]

        @pl.when(pl.program_id(2) == 0)
        def _():
            kf_ref[:nl, :] = prep(kl_ref[...], kg_ref, tuple(r[...] for r in ktab_refs))
            kf_ref[nl:, :] = prep(kc_ref[...], kg_ref, None)
            vf_ref[:nl, :] = vl_ref[...]
            vf_ref[nl:, :] = vc_ref[...]

        k, v = kf_ref[...], vf_ref[...]
    else:
        k, v = prep(kc_ref[...], kg_ref, None), vc_ref[...]
    for g in range(group):
        sl = slice(g * HEAD_DIM, (g + 1) * HEAD_DIM)
        q = prep(q_ref[:, sl], qg_ref, qtabs)
        o_ref[:, sl] = _softmax_pv(_dot_nt(q, k), v).astype(o_ref.dtype)


def _gattn(q_arr, q_col, k_arr, k_col, v_arr, v_col, *, group, n_kv, nb, seq, ctx, with_lat, out_rows, name,
           gains=None, tabs=None):
    nlat_c = nb * seq // ctx
    rms = gains is not None
    if with_lat:
        tq = _tile((512, 256, 128), seq)
        nq = seq // tq
        q_map = lambda b, h, i: (b * nq + i, q_col + h)
    else:
        tq, nq = ctx, 1
        q_map = lambda b, h, i: (nlat_c + b, q_col + h)
    gw = group * HEAD_DIM
    specs = [pl.BlockSpec((tq, gw), q_map)]
    ins = [q_arr]
    if with_lat:
        specs += [pl.BlockSpec((seq, HEAD_DIM), lambda b, h, i: (b, k_col + h)),
                  pl.BlockSpec((seq, HEAD_DIM), lambda b, h, i: (b, v_col + h))]
        ins += [k_arr, v_arr]
    specs += [pl.BlockSpec((ctx, HEAD_DIM), lambda b, h, i: (nlat_c + b, k_col + h)),
              pl.BlockSpec((ctx, HEAD_DIM), lambda b, h, i: (nlat_c + b, v_col + h))]
    ins += [k_arr, v_arr]
    if rms:
        specs += [pl.BlockSpec((1, HEAD_DIM), lambda b, h, i: (0, 0))] * 2
        ins += list(gains)
    scratch = []
    if with_lat:
        specs += [pl.BlockSpec((tq, LANES), lambda b, h, i: (i, 0))] * 3
        specs += [pl.BlockSpec((seq, LANES), lambda b, h, i: (0, 0))] * 3
        ins += list(tabs) * 2
        scratch = [pltpu.VMEM((seq + ctx, HEAD_DIM), BF16)] * 2
    o_map = (lambda b, h, i: (b * nq + i, h)) if with_lat else (lambda b, h, i: (b, h))
    return pl.pallas_call(
        functools.partial(_gattn_kernel, group=group, with_lat=with_lat, rms=rms),
        grid=(nb, n_kv, nq), in_specs=specs,
        out_specs=pl.BlockSpec((tq, gw), o_map),
        out_shape=jax.ShapeDtypeStruct((out_rows, n_kv * gw), BF16),
        scratch_shapes=scratch,
        compiler_params=_cp("parallel", "parallel", "arbitrary"), name=name,
    )(*ins)


def _mla_kernel(*refs, with_lat):
    if with_lat:
        qn_ref, qr_ref, kvl_ref, krl_ref, kvc_ref, krc_ref = refs[:6]
        qtabs = tuple(r[...] for r in refs[6:9])
        ktab_refs = refs[9:12]
        o_ref, kf_ref, vf_ref = refs[12:]
        nl = kvl_ref.shape[0]

        @pl.when(pl.program_id(2) == 0)
        def _():
            kr = _rope(krl_ref[...].astype(F32), tuple(r[...] for r in ktab_refs), MLA_ROPE // 4).astype(BF16)
            for hh in range(2):
                c0 = hh * 2 * HEAD_DIM
                kf_ref[hh, :nl, :MLA_NOPE] = kvl_ref[:, c0:c0 + MLA_NOPE]
                kf_ref[hh, :nl, MLA_NOPE:] = kr
                kf_ref[hh, nl:, :MLA_NOPE] = kvc_ref[:, c0:c0 + MLA_NOPE]
                kf_ref[hh, nl:, MLA_NOPE:] = krc_ref[...]
                vf_ref[hh, :nl, :] = kvl_ref[:, c0 + MLA_NOPE:c0 + 2 * HEAD_DIM]
                vf_ref[hh, nl:, :] = kvc_ref[:, c0 + MLA_NOPE:c0 + 2 * HEAD_DIM]

        qr = _rope(qr_ref[...].astype(F32), qtabs, MLA_ROPE // 4)
    else:
        qn_ref, qr_ref, kvc_ref, krc_ref, o_ref = refs
        qr = qr_ref[...].astype(F32)
    first_half = lax.broadcasted_iota(jnp.int32, qr.shape, 1) < MLA_ROPE
    for hh in range(2):
        c0 = hh * 2 * HEAD_DIM
        own = first_half if hh == 0 else jnp.logical_not(first_half)
        q = jnp.concatenate([qn_ref[:, hh * MLA_NOPE:(hh + 1) * MLA_NOPE],
                             jnp.where(own, qr, 0.0).astype(BF16)], axis=1)
        if with_lat:
            k, v = kf_ref[hh], vf_ref[hh]
        else:
            k = jnp.concatenate([kvc_ref[:, c0:c0 + MLA_NOPE], krc_ref[...]], axis=1)
            v = kvc_ref[:, c0 + MLA_NOPE:c0 + 2 * HEAD_DIM]
        o_ref[:, hh * HEAD_DIM:(hh + 1) * HEAD_DIM] = _softmax_pv(_dot_nt(q, k), v).astype(o_ref.dtype)


def _mla(p_arr, qn_col, qr_col, kr_col, kv_arr, *, heads, nb, seq, ctx, with_lat, out_rows, name, tabs=None):
    nlat_c = nb * seq // ctx
    if with_lat:
        tq = _tile((512, 256, 128), seq)
        nq = seq // tq
        row = lambda b, i: b * nq + i
    else:
        tq, nq = ctx, 1
        row = lambda b, i: nlat_c + b
    specs = [pl.BlockSpec((tq, 2 * MLA_NOPE), lambda b, h, i: (row(b, i), qn_col + h)),
             pl.BlockSpec((tq, 2 * MLA_ROPE), lambda b, h, i: (row(b, i), qr_col + h))]
    ins = [p_arr, p_arr]
    if with_lat:
        specs += [pl.BlockSpec((seq, 4 * HEAD_DIM), lambda b, h, i: (b, h)),
                  pl.BlockSpec((seq, LANES), lambda b, h, i: (b, kr_col))]
        ins += [kv_arr, p_arr]
    specs += [pl.BlockSpec((ctx, 4 * HEAD_DIM), lambda b, h, i: (nlat_c + b, h)),
              pl.BlockSpec((ctx, LANES), lambda b, h, i: (nlat_c + b, kr_col))]
    ins += [kv_arr, p_arr]
    scratch = []
    if with_lat:
        specs += [pl.BlockSpec((tq, LANES), lambda b, h, i: (i, 0))] * 3
        specs += [pl.BlockSpec((seq, LANES), lambda b, h, i: (0, 0))] * 3
        ins += list(tabs) * 2
        scratch = [pltpu.VMEM((2, seq + ctx, 2 * HEAD_DIM), BF16), pltpu.VMEM((2, seq + ctx, HEAD_DIM), BF16)]
    o_map = (lambda b, h, i: (b * nq + i, h)) if with_lat else (lambda b, h, i: (b, h))
    return pl.pallas_call(
        functools.partial(_mla_kernel, with_lat=with_lat),
        grid=(nb, heads // 2, nq), in_specs=specs,
        out_specs=pl.BlockSpec((tq, 2 * HEAD_DIM), o_map),
        out_shape=jax.ShapeDtypeStruct((out_rows, heads * HEAD_DIM), BF16),
        scratch_shapes=scratch,
        compiler_params=_cp("parallel", "parallel", "arbitrary"), name=name,
    )(*ins)


def _na_fill_bias(w_ref, bias_ref):
    cq = lax.broadcasted_iota(jnp.int32, (GRID_W, LANES), 0)
    lane = lax.broadcasted_iota(jnp.int32, (GRID_W, LANES), 1)
    ck = lane % GRID_W
    cs = jnp.clip(cq - NA_KC // 2, 0, GRID_W - NA_KC)
    col_ok = (ck >= cs) & (ck < cs + NA_KC)
    even = lane < GRID_W
    ninf = jnp.full((GRID_W, LANES), NEG_INF, F32)
    w = w_ref[...]
    toe = {}

    def toeplitz(a, odd):
        if (a, odd) not in toe:
            wa = jnp.broadcast_to(w[a:a + 1, :], (GRID_W, LANES))
            toe[(a, odd)] = pltpu.roll(wa, GRID_W if odd else 0, 1, stride=1, stride_axis=0)
        return toe[(a, odd)]

    for kind, r0_rel in enumerate((0, NA_KR // 2, NA_WROWS - NA_QROWS)):
        for rq in range(NA_QROWS):
            rs_rel = (0, rq, NA_WROWS - NA_KR)[kind]
            for pair in range(NA_WROWS // 2):
                halves = []
                for odd in (0, 1):
                    rk = 2 * pair + odd
                    if rs_rel <= rk < rs_rel + NA_KR:
                        halves.append(toeplitz(rk - r0_rel - rq + NA_KR - 1, odd))
                    else:
                        halves.append(ninf)
                blk = jnp.where(col_ok, jnp.where(even, halves[0], halves[1]), NEG_INF)
                bias_ref[kind, rq * GRID_W:(rq + 1) * GRID_W, pair * LANES:(pair + 1) * LANES] = blk


def _na_kernel(q_ref, k_ref, v_ref, kc_ref, vc_ref, w_ref, o_ref, bias_ref, *, nblk):
    qt = NA_QROWS * GRID_W
    wt = NA_WROWS * GRID_W

    @pl.when(pl.program_id(1) == 0)
    def _():
        _na_fill_bias(w_ref, bias_ref)

    kc, vc = kc_ref[...], vc_ref[...]
    for rb in range(nblk):
        q0 = rb * qt
        w0 = min(max(rb - 1, 0), nblk - 3) * qt
        kind = 0 if rb == 0 else (2 if rb == nblk - 1 else 1)
        q = q_ref[q0:q0 + qt, :]
        s_lat = _dot_nt(q, k_ref[w0:w0 + wt, :]) + bias_ref[kind]
        s_ctx = _dot_nt(q, kc)
        m = jnp.maximum(jnp.max(s_lat, -1, keepdims=True), jnp.max(s_ctx, -1, keepdims=True))
        e_lat, e_ctx = jnp.exp(s_lat - m), jnp.exp(s_ctx - m)
        den = jnp.sum(e_lat, -1, keepdims=True) + jnp.sum(e_ctx, -1, keepdims=True)
        o = _dot(e_lat.astype(BF16), v_ref[w0:w0 + wt, :]) + _dot(e_ctx.astype(BF16), vc)
        o_ref[q0:q0 + qt, :] = (o * (1.0 / den)).astype(o_ref.dtype)


def _na(p_arr, q_col, rpb_w, *, heads, nb, seq, ctx, out_rows):
    nlat_c = nb * seq // ctx
    nblk = seq // (NA_QROWS * GRID_W)
    qt, wt = NA_QROWS * GRID_W, NA_WROWS * GRID_W
    lat = lambda off: pl.BlockSpec((seq, HEAD_DIM), lambda h, b: (b, q_col + off + h))
    cx = lambda off: pl.BlockSpec((ctx, HEAD_DIM), lambda h, b: (nlat_c + b, q_col + off + h))
    return pl.pallas_call(
        functools.partial(_na_kernel, nblk=nblk),
        grid=(heads, nb),
        in_specs=[lat(0), lat(heads), lat(2 * heads), cx(heads), cx(2 * heads),
                  pl.BlockSpec((None,) + rpb_w.shape[1:], lambda h, b: (h, 0, 0))],
        out_specs=pl.BlockSpec((seq, HEAD_DIM), lambda h, b: (b, h)),
        out_shape=jax.ShapeDtypeStruct((out_rows, heads * HEAD_DIM), BF16),
        scratch_shapes=[pltpu.VMEM((3, qt, wt), F32)],
        compiler_params=_cp("parallel", "arbitrary"), name="na_lat",
    )(p_arr, p_arr, p_arr, p_arr, p_arr, rpb_w)


def _merge_kernel(ya_ref, yb_ref, yc_ref, w_ref, ga_ref, gb_ref, gc_ref, o_ref):
    acc = None
    for i, (y_ref, g_ref) in enumerate(((ya_ref, ga_ref), (yb_ref, gb_ref), (yc_ref, gc_ref))):
        t = jax.nn.sigmoid(g_ref[...].astype(F32)) * _dot(y_ref[...], w_ref[i])
        acc = t if acc is None else acc + t
    o_ref[...] = acc.astype(o_ref.dtype)


def _merge(ya, yb, yc, w_branch, p_arr, gate_col, *, rows):
    d = ya.shape[1]
    n = w_branch.shape[2]
    tm = _tile((512, 256, 128), rows)
    tn = _tile((512, 256, 128), n)
    y_spec = pl.BlockSpec((tm, d), lambda i, j: (i, 0))
    g_spec = lambda b: pl.BlockSpec((tm, tn), lambda i, j: (i, (gate_col + b * n) // tn + j))
    return pl.pallas_call(
        _merge_kernel, grid=(rows // tm, n // tn),
        in_specs=[y_spec, y_spec, y_spec, pl.BlockSpec((3, d, tn), lambda i, j: (0, 0, j)),
                  g_spec(0), g_spec(1), g_spec(2)],
        out_specs=pl.BlockSpec((tm, tn), lambda i, j: (i, j)),
        out_shape=jax.ShapeDtypeStruct((rows, n), BF16),
        compiler_params=_cp("parallel", "parallel"), name="merge",
    )(ya, yb, yc, w_branch, p_arr, p_arr, p_arr)


def _post_ln_epilogue(y, x, m, gate_row, g, b, alpha):
    z = alpha * x + m[gate_row:gate_row + 1] * y
    return _ln(z, POST_EPS) * g + b


def _out_ln_kernel(a_ref, w_ref, x_ref, m_ref, g_ref, b_ref, xo_ref, h_ref, *, alpha):
    m = m_ref[...]
    xn = _post_ln_epilogue(_dot(a_ref[...], w_ref[...]), x_ref[...], m, 2, g_ref[...], b_ref[...], alpha)
    xo_ref[...] = xn
    h_ref[...] = (_ln(xn, ADA_EPS) * (1.0 + m[4:5]) + m[3:4]).astype(BF16)


def _out_ln(a, w_out, x, mods, ln_g, ln_b, *, rows, seq, nb, alpha):
    d = x.shape[1]
    tm = _tile((256, 128), rows, seq)
    row = pl.BlockSpec((tm, d), lambda i: (i, 0))
    vec = pl.BlockSpec((1, d), lambda i: (0, 0))
    return pl.pallas_call(
        functools.partial(_out_ln_kernel, alpha=alpha), grid=(rows // tm,),
        in_specs=[row, pl.BlockSpec((d, d), lambda i: (0, 0)), row,
                  pl.BlockSpec((None, 6, d), lambda i: (jnp.minimum(i * tm // seq, nb), 0, 0)), vec, vec],
        out_specs=[row, row],
        out_shape=[jax.ShapeDtypeStruct((rows, d), F32), jax.ShapeDtypeStruct((rows, d), BF16)],
        compiler_params=_cp("parallel"), name="out_ln",
    )(a, w_out, x, mods, ln_g, ln_b)


def _conv_kernel(g_ref, v_ref, gp_ref, gn_ref, cw_ref, cb_ref, o_ref, buf_ref, *, tm, seq, ctx, nlat):
    base = pl.program_id(0) * tm
    in_lat = base < nlat
    pos0 = jnp.where(in_lat, base % seq, (base - nlat) % ctx)
    seq_len = jnp.where(in_lat, seq, ctx)
    g = g_ref[...].astype(F32)
    prev_row = gp_ref[SUBLANES - 1:SUBLANES, :].astype(F32)
    next_row = gn_ref[0:1, :].astype(F32)
    buf_ref[SUBLANES:SUBLANES + tm, :] = g
    buf_ref[SUBLANES - 1:SUBLANES, :] = jnp.where(pos0 == 0, 0.0, prev_row)
    buf_ref[SUBLANES + tm:SUBLANES + tm + 1, :] = jnp.where(pos0 + tm == seq_len, 0.0, next_row)
    cw = cw_ref[...]
    gc = (buf_ref[SUBLANES - 1:SUBLANES - 1 + tm, :] * cw[0:1] + g * cw[1:2]
          + buf_ref[SUBLANES + 1:SUBLANES + 1 + tm, :] * cw[2:3] + cb_ref[...])
    o_ref[...] = (gc * jax.nn.sigmoid(gc) * v_ref[...].astype(F32)).astype(o_ref.dtype)


def _conv_gate(u, conv_w, conv_b, *, rows, seq, ctx, nlat):
    dff = conv_w.shape[1]
    tm = _tile((512, 256, 128), rows, ctx)
    tf = _tile((512, 256, 128), dff)
    nf = dff // tf
    hb = tm // SUBLANES
    last_hb = rows // SUBLANES - 1
    return pl.pallas_call(
        functools.partial(_conv_kernel, tm=tm, seq=seq, ctx=ctx, nlat=nlat),
        grid=(rows // tm, nf),
        in_specs=[pl.BlockSpec((tm, tf), lambda i, j: (i, j)),
                  pl.BlockSpec((tm, tf), lambda i, j: (i, nf + j)),
                  pl.BlockSpec((SUBLANES, tf), lambda i, j: (jnp.maximum(i * hb - 1, 0), j)),
                  pl.BlockSpec((SUBLANES, tf), lambda i, j: (jnp.minimum((i + 1) * hb, last_hb), j)),
                  pl.BlockSpec((3, tf), lambda i, j: (0, j)),
                  pl.BlockSpec((1, tf), lambda i, j: (0, j))],
        out_specs=pl.BlockSpec((tm, tf), lambda i, j: (i, j)),
        out_shape=jax.ShapeDtypeStruct((rows, dff), BF16),
        scratch_shapes=[pltpu.VMEM((tm + 2 * SUBLANES, tf), F32)],
        compiler_params=_cp("parallel", "parallel"), name="conv_gate",
    )(u, u, u, u, conv_w, conv_b)


def _down_ln_kernel(*refs, alpha, nk, with_next):
    if with_next:
        a_ref, w_ref, x_ref, m_ref, g_ref, b_ref, mn_ref, xo_ref, h_ref, acc_ref = refs
    else:
        a_ref, w_ref, x_ref, m_ref, g_ref, b_ref, xo_ref, acc_ref = refs
    k = pl.program_id(1)

    @pl.when(k == 0)
    def _():
        acc_ref[...] = jnp.zeros_like(acc_ref)

    acc_ref[...] += _dot(a_ref[...], w_ref[...])

    @pl.when(k == nk - 1)
    def _():
        xn = _post_ln_epilogue(acc_ref[...], x_ref[...], m_ref[...], 5, g_ref[...], b_ref[...], alpha)
        xo_ref[...] = xn
        if with_next:
            mn = mn_ref[...]
            h_ref[...] = (_ln(xn, ADA_EPS) * (1.0 + mn[1:2]) + mn[0:1]).astype(BF16)


def _down_ln(a, w_down, x, mods, ln_g, ln_b, mods_next, *, rows, seq, nb, alpha):
    d = x.shape[1]
    dff = a.shape[1]
    tm = _tile((512, 256, 128), rows, seq)
    nk = next(c for c in (4, 2, 1, 8, 11, 22, 44) if dff % c == 0 and (dff // c) % LANES == 0)
    tk = dff // nk
    with_next = mods_next is not None
    row = pl.BlockSpec((tm, d), lambda i, k: (i, 0))
    vec = pl.BlockSpec((1, d), lambda i, k: (0, 0))
    mod = pl.BlockSpec((None, 6, d), lambda i, k: (jnp.minimum(i * tm // seq, nb), 0, 0))
    specs = [pl.BlockSpec((tm, tk), lambda i, k: (i, k)), pl.BlockSpec((tk, d), lambda i, k: (k, 0)),
             row, mod, vec, vec]
    ins = [a, w_down, x, mods, ln_g, ln_b]
    out_specs = [row]
    out_shape = [jax.ShapeDtypeStruct((rows, d), F32)]
    if with_next:
        specs.append(mod)
        ins.append(mods_next)
        out_specs.append(row)
        out_shape.append(jax.ShapeDtypeStruct((rows, d), BF16))
    res = pl.pallas_call(
        functools.partial(_down_ln_kernel, alpha=alpha, nk=nk, with_next=with_next),
        grid=(rows // tm, nk), in_specs=specs, out_specs=out_specs, out_shape=out_shape,
        scratch_shapes=[pltpu.VMEM((tm, d), F32)],
        compiler_params=_cp("parallel", "arbitrary"), name="down_ln",
    )(*ins)
    return (res[0], res[1]) if with_next else (res[0], None)


def _rope_tables(seq, dims, half):
    t = jnp.arange(seq)
    rows, cols = (t // GRID_W).astype(F32), (t % GRID_W).astype(F32)
    j = jnp.arange(LANES)
    freqs = ROPE_THETA ** (-jnp.arange(half, dtype=F32) / half)
    f = freqs[j % half]
    use_row = (j % dims) < dims // 2
    ang = jnp.where(use_row[None, :], rows[:, None], cols[:, None]) * f[None, :]
    cos, sin = jnp.cos(ang), jnp.sin(ang)
    upper = (j % (2 * half)) < half
    return cos, jnp.where(upper[None, :], -sin, 0.0), jnp.where(upper[None, :], 0.0, sin)


def _rpb_rows(rpb):
    h, nr, nc = rpb.shape
    w = jnp.pad(rpb.astype(F32), ((0, 0), (0, 2 * SUBLANES - nr), (0, LANES - nc)))
    return jnp.roll(w, -(NA_KC - 1), axis=2)


def kernel(x, c, ctx, c_ctx, w_ada, b_ada, w_in, mla_kv_norm, w_mla_ukv, gqa_q_norm, gqa_k_norm, na_rpb,
           w_branch, w_out, ln_a_g, ln_a_b, w_up, conv_w, conv_b, w_down, ln_f_g, ln_f_b):
    nb, seq, d = x.shape
    lctx = ctx.shape[1]
    depth = w_in.shape[0]
    heads = d // HEAD_DIM
    gkv = heads // 4
    group = heads // gkv
    kvr = mla_kv_norm.shape[1]
    nlat, nctx = nb * seq, nb * lctx
    t_all = nlat + nctx
    alpha = float((2 * depth) ** 0.25)
    mla_scale = float((MLA_NOPE + MLA_ROPE) ** -0.5)
    hd_scale = float(HEAD_DIM ** -0.5)
    assert seq % (NA_QROWS * GRID_W) == 0 and seq // (NA_QROWS * GRID_W) >= 3
    assert heads % 4 == 0 and seq % lctx == 0 and nctx <= seq

    o_ckv = heads * (MLA_NOPE + MLA_ROPE)
    o_kr = o_ckv + kvr
    o_na = o_kr + MLA_ROPE
    o_gq = o_na + 3 * heads * HEAD_DIM
    o_gkv = o_gq + heads * HEAD_DIM
    o_gate = o_gkv + 2 * gkv * HEAD_DIM

    pad = (-(nb + 1)) % SUBLANES
    c_all = jnp.concatenate([c, c_ctx[None, :], jnp.zeros((pad, d), F32)], 0)
    mods = _ada(c_all, w_ada, b_ada).reshape(depth, nb + 1 + pad, 6, d)

    tm_proj = _tile((1024, 512, 256, 128), seq, nctx)
    rope_g = _rope_tables(seq, HEAD_DIM, HEAD_DIM // 4)
    rope_m = _rope_tables(seq, MLA_ROPE, MLA_ROPE // 4)

    xs = jnp.concatenate([x.reshape(nlat, d), ctx.reshape(nctx, d)], 0)
    h = _ln_mod(xs, mods[0], seq, nb)

    c_gq, c_naq, c_gate = 0, heads, 4 * heads
    c_mqn = c_gate + 3 * heads
    c_gv = c_mqn + heads
    c_gk = c_gv + gkv
    c_mqr = c_gk + gkv
    c_kr = c_mqr + heads // 2
    ones = lambda nblk: jnp.ones((nblk * HEAD_DIM,), F32)
    cs_proj = jnp.concatenate([ones(heads), ones(heads) * hd_scale, ones(5 * heads), ones(heads) * mla_scale,
                               ones(2 * gkv), ones(heads // 2) * mla_scale, ones(1)])[None, :]

    for l in range(depth):
        last = l == depth - 1
        rows = nlat if last else t_all
        wl = w_in[l]
        mq = wl[:, :o_ckv].reshape(d, heads, MLA_NOPE + MLA_ROPE)
        gkvw = wl[:, o_gkv:o_gate].reshape(d, 2, gkv * HEAD_DIM)
        w_proj = jnp.concatenate(
            [wl[:, o_gq:o_gkv], wl[:, o_na:o_gq], wl[:, o_gate:], mq[:, :, :MLA_NOPE].reshape(d, heads * MLA_NOPE),
             gkvw[:, 1], gkvw[:, 0], mq[:, :, MLA_NOPE:].reshape(d, heads * MLA_ROPE),
             wl[:, o_kr:o_na], wl[:, o_kr:o_na]], 1).astype(BF16)

        p = _mm(h, w_proj, tm=tm_proj, cs=cs_proj, name="proj")
        p_ckv = _mm(h, wl[:, o_ckv:o_kr].astype(BF16), tm=tm_proj, tn=kvr, rms_gain=mla_kv_norm[l][None, :],
                    name="proj_ckv")
        p_kv = _mm(p_ckv, w_mla_ukv[l].astype(BF16), tm=tm_proj, name="mla_ukv")

        dims = dict(nb=nb, seq=seq, ctx=lctx)
        gains = ((gqa_q_norm[l] * hd_scale)[None, :], gqa_k_norm[l][None, :])
        ya = _mla(p, c_mqn // 2, c_mqr, c_kr, p_kv, heads=heads, with_lat=True, out_rows=rows, name="mla_lat",
                  tabs=rope_m, **dims)
        yb = _na(p, c_naq, _rpb_rows(na_rpb[l]), heads=heads, out_rows=rows, **dims)
        yc = _gattn(p, c_gq // group, p, c_gk, p, c_gv, group=group, n_kv=gkv, with_lat=True, out_rows=rows,
                    name="gqa_lat", gains=gains, tabs=rope_g, **dims)
        if not last:
            ya_c = _mla(p, c_mqn // 2, c_mqr, c_kr, p_kv, heads=heads, with_lat=False, out_rows=nctx,
                        name="mla_ctx", **dims)
            yb_c = _gattn(p, c_naq, p, c_naq + heads, p, c_naq + 2 * heads, group=1, n_kv=heads, with_lat=False,
                          out_rows=nctx, name="na_ctx", **dims)
            yc_c = _gattn(p, c_gq // group, p, c_gk, p, c_gv, group=group, n_kv=gkv, with_lat=False,
                          out_rows=nctx, name="gqa_ctx", gains=gains, **dims)
            ya = lax.dynamic_update_slice(ya, ya_c, (nlat, 0))
            yb = lax.dynamic_update_slice(yb, yb_c, (nlat, 0))
            yc = lax.dynamic_update_slice(yc, yc_c, (nlat, 0))

        merged = _merge(ya, yb, yc, w_branch[l].astype(BF16), p, c_gate * HEAD_DIM, rows=rows)
        xs, h2 = _out_ln(merged, w_out[l].astype(BF16), xs, mods[l], ln_a_g[l][None, :], ln_a_b[l][None, :],
                         rows=rows, seq=seq, nb=nb, alpha=alpha)
        u = _mm(h2, w_up[l].astype(BF16), rows=rows, name="ffn_up")
        hid = _conv_gate(u, conv_w[l], conv_b[l][None, :], rows=rows, seq=seq, ctx=lctx, nlat=nlat)
        xs, h = _down_ln(hid, w_down[l].astype(BF16), xs, mods[l], ln_f_g[l][None, :], ln_f_b[l][None, :],
                         None if last else mods[l + 1], rows=rows, seq=seq, nb=nb, alpha=alpha)
    return xs[:nlat].reshape(nb, seq, d)
```

```python
import functools
import math

import jax
import jax.numpy as jnp
from jax import lax
from jax.experimental import pallas as pl
from jax.experimental.pallas import tpu as pltpu

F32 = jnp.float32
BF16 = jnp.bfloat16

GRID_W = 64
HEAD_DIM = 128
MLA_NOPE = 128
MLA_ROPE = 64
NA_KR = 8
NA_KC = 16
NA_QROWS = 4
NA_WROWS = NA_QROWS + NA_KR
ROPE_THETA = 10000.0
ADA_EPS = 1e-6
POST_EPS = 1e-5
RMS_EPS = 1e-6
NEG_INF = -1e30
LOG2E = math.log2(math.e)
LANES = 128
SUBLANES = 8
VMEM_LIMIT = 56 * 2**20
ATTN_TQ = 256


def _cp(*sem):
    return pltpu.CompilerParams(dimension_semantics=sem, vmem_limit_bytes=VMEM_LIMIT)


def _tile(cands, *ns):
    for c in cands:
        if all(n % c == 0 for n in ns):
            return c
    raise ValueError(f"no tile in {cands} divides {ns}")


def _ln(x, eps):
    mu = jnp.mean(x, -1, keepdims=True)
    xc = x - mu
    var = jnp.mean(xc * xc, -1, keepdims=True)
    return xc * lax.rsqrt(var + eps)


def _rms(x):
    return x * lax.rsqrt(jnp.mean(x * x, -1, keepdims=True) + RMS_EPS)


def _rope(x, tabs, shift):
    cos, sa, sb = tabs
    return x * cos + pltpu.roll(x, LANES - shift, 1) * sa + pltpu.roll(x, shift, 1) * sb


def _dot(a, b):
    return jnp.dot(a, b, preferred_element_type=F32)


def _dot_nt(a, b):
    return lax.dot_general(a, b, (((1,), (1,)), ((), ())), preferred_element_type=F32)


def _ada_kernel(c_ref, w_ref, b_ref, o_ref):
    c = c_ref[...]
    a = (c * jax.nn.sigmoid(c)).astype(BF16)
    o_ref[...] = _dot(a, w_ref[...].astype(BF16)) + b_ref[...]


def _ada(c_all, w_ada, b_ada):
    depth, d, n = w_ada.shape
    r = c_all.shape[0]
    tn = _tile((1024, 512, 256, 128), n)
    return pl.pallas_call(
        _ada_kernel,
        grid=(depth, n // tn),
        in_specs=[pl.BlockSpec((r, d), lambda l, j: (0, 0)),
                  pl.BlockSpec((None, d, tn), lambda l, j: (l, 0, j)),
                  pl.BlockSpec((None, 1, tn), lambda l, j: (l, 0, j))],
        out_specs=pl.BlockSpec((None, r, tn), lambda l, j: (l, 0, j)),
        out_shape=jax.ShapeDtypeStruct((depth, r, n), F32),
        compiler_params=_cp("parallel", "parallel"),
        name="ada",
    )(c_all, w_ada, b_ada.reshape(depth, 1, n))


def _ln_mod_kernel(x_ref, m_ref, h_ref):
    m = m_ref[...]
    h_ref[...] = (_ln(x_ref[...], ADA_EPS) * (1.0 + m[1:2]) + m[0:1]).astype(BF16)


def _ln_mod(x, mods, seq, nb):
    t, d = x.shape
    tm = _tile((512, 256, 128), seq, t)
    return pl.pallas_call(
        _ln_mod_kernel,
        grid=(t // tm,),
        in_specs=[pl.BlockSpec((tm, d), lambda i: (i, 0)),
                  pl.BlockSpec((None, 6, d), lambda i: (jnp.minimum(i * tm // seq, nb), 0, 0))],
        out_specs=pl.BlockSpec((tm, d), lambda i: (i, 0)),
        out_shape=jax.ShapeDtypeStruct((t, d), BF16),
        compiler_params=_cp("parallel"),
        name="ln_mod",
    )(x, mods)


def _mm_kernel(a_ref, w_ref, o_ref):
    o_ref[...] = _dot(a_ref[...], w_ref[...]).astype(o_ref.dtype)


def _mm_scale_kernel(a_ref, w_ref, cs_ref, o_ref):
    o_ref[...] = (_dot(a_ref[...], w_ref[...]) * cs_ref[...]).astype(o_ref.dtype)


def _mm(a, w, *, rows=None, tm=None, tn=None, cs=None, name="mm"):
    t, k = a.shape
    n = w.shape[1]
    rows = t if rows is None else rows
    tm = tm or _tile((1024, 512, 256, 128), rows)
    tn = tn or _tile((512, 256, 128), n)
    a_spec = pl.BlockSpec((tm, k), lambda i, j: (i, 0))
    w_spec = pl.BlockSpec((k, tn), lambda i, j: (0, j))
    if cs is not None:
        kern, ins, specs = _mm_scale_kernel, (a, w, cs), [a_spec, w_spec, pl.BlockSpec((1, tn), lambda i, j: (0, j))]
    else:
        kern, ins, specs = _mm_kernel, (a, w), [a_spec, w_spec]
    return pl.pallas_call(
        kern, grid=(rows // tm, n // tn), in_specs=specs,
        out_specs=pl.BlockSpec((tm, tn), lambda i, j: (i, j)),
        out_shape=jax.ShapeDtypeStruct((rows, n), BF16),
        compiler_params=_cp("parallel", "parallel"), name=name,
    )(*ins)


def _ckv_kernel(a_ref, wc_ref, g_ref, wr_ref, oc_ref, or_ref):
    a = a_ref[...]
    oc_ref[...] = (_rms(_dot(a, wc_ref[...])) * g_ref[...]).astype(oc_ref.dtype)
    or_ref[...] = _dot(a, wr_ref[...]).astype(or_ref.dtype)


def _proj_ckv(a, w_ckv, gain, w_kr2, *, tm):
    t, k = a.shape
    r = w_ckv.shape[1]
    full = lambda shape: pl.BlockSpec(shape, lambda i: (0, 0))
    return pl.pallas_call(
        _ckv_kernel, grid=(t // tm,),
        in_specs=[pl.BlockSpec((tm, k), lambda i: (i, 0)), full((k, r)), full((1, r)), full((k, LANES))],
        out_specs=[pl.BlockSpec((tm, r), lambda i: (i, 0)), pl.BlockSpec((tm, LANES), lambda i: (i, 0))],
        out_shape=[jax.ShapeDtypeStruct((t, r), BF16), jax.ShapeDtypeStruct((t, LANES), BF16)],
        compiler_params=_cp("parallel"), name="proj_ckv",
    )(a, w_ckv, gain, w_kr2)


def _softmax_pv(s, v):
    e = jnp.exp2(s - jnp.max(s, -1, keepdims=True))
    den = jnp.sum(e, -1, keepdims=True)
    return _dot(e.astype(BF16), v) * (1.0 / den)


def _softmax_pv_ones(s, v_ext):
    e = jnp.exp2(s - jnp.max(s, -1, keepdims=True)).astype(BF16)
    o = _dot(e, v_ext)
    return o[:, :HEAD_DIM] * (1.0 / o[:, HEAD_DIM:HEAD_DIM + 1])


def _stage_values(vf_ref, v_lat, v_ctx):
    nl = v_lat.shape[0]
    vf_ref[:nl, :HEAD_DIM] = v_lat
    vf_ref[nl:, :HEAD_DIM] = v_ctx
    lane = lax.broadcasted_iota(jnp.int32, (vf_ref.shape[0], LANES), 1)
    vf_ref[:, HEAD_DIM:] = jnp.where(lane == 0, 1.0, 0.0).astype(vf_ref.dtype)


def _chunk_rows(i, tq):
    return pl.ds(i * tq, tq) if isinstance(i, int) else pl.ds(pl.multiple_of(i * tq, tq), tq)


def _pipelined_chunks(nq, qk, sm_pv):
    qk(0, 0)
    npairs = (nq - 1) // 2

    def body(j, carry):
        i = 2 * j
        qk(i + 1, 1)
        sm_pv(i, 0)
        qk(i + 2, 0)
        sm_pv(i + 1, 1)
        return carry

    lax.fori_loop(0, npairs, body, 0)
    i = 2 * npairs
    if i + 1 < nq:
        qk(i + 1, 1)
        sm_pv(i, 0)
        sm_pv(i + 1, 1)
    else:
        sm_pv(i, 0)


def _gattn_lat_kernel(q_ref, kl_ref, vl_ref, kc_ref, vc_ref, qg_ref, kg_ref, cos_ref, sa_ref, sb_ref,
                      o_ref, kf_ref, vf_ref, s0_ref, s1_ref, *, group, tq):
    nl = kl_ref.shape[0]

    def prep(x, g_ref, tabs):
        x = _rms(x.astype(F32)) * g_ref[...]
        if tabs is not None:
            x = _rope(x, tabs, HEAD_DIM // 4)
        return x.astype(BF16)

    kf_ref[:nl, :] = prep(kl_ref[...], kg_ref, (cos_ref[...], sa_ref[...], sb_ref[...]))
    kf_ref[nl:, :] = prep(kc_ref[...], kg_ref, None)
    _stage_values(vf_ref, vl_ref[...], vc_ref[...])

    s_refs = (s0_ref, s1_ref)

    def qk(i, slot):
        rows = _chunk_rows(i, tq)
        tabs = (cos_ref[rows, :], sa_ref[rows, :], sb_ref[rows, :])
        for g in range(group):
            q = prep(q_ref[rows, g * HEAD_DIM:(g + 1) * HEAD_DIM], qg_ref, tabs)
            s_refs[slot][g] = _dot_nt(q, kf_ref[...])

    def sm_pv(i, slot):
        rows = _chunk_rows(i, tq)
        for g in range(group):
            o = _softmax_pv_ones(s_refs[slot][g], vf_ref[...])
            o_ref[rows, g * HEAD_DIM:(g + 1) * HEAD_DIM] = o.astype(o_ref.dtype)

    _pipelined_chunks(nl // tq, qk, sm_pv)


def _gattn_ctx_kernel(*refs, group, rms):
    if rms:
        q_ref, kc_ref, vc_ref, qg_ref, kg_ref, o_ref = refs
        norm = lambda x, g_ref: (_rms(x.astype(F32)) * g_ref[...]).astype(BF16)
        k = norm(kc_ref[...], kg_ref)
    else:
        q_ref, kc_ref, vc_ref, o_ref = refs
        k = kc_ref[...]
    for g in range(group):
        sl = slice(g * HEAD_DIM, (g + 1) * HEAD_DIM)
        q = norm(q_ref[:, sl], qg_ref) if rms else q_ref[:, sl]
        o_ref[:, sl] = _softmax_pv(_dot_nt(q, k), vc_ref[...]).astype(o_ref.dtype)


def _gattn(q_arr, q_col, k_arr, k_col, v_arr, v_col, *, group, n_kv, nb, seq, ctx, with_lat, out_rows, name,
           gains=None, tabs=None):
    nlat_c = nb * seq // ctx
    gw = group * HEAD_DIM
    kc_spec = pl.BlockSpec((ctx, HEAD_DIM), lambda b, h: (nlat_c + b, k_col + h))
    vc_spec = pl.BlockSpec((ctx, HEAD_DIM), lambda b, h: (nlat_c + b, v_col + h))
    g_spec = pl.BlockSpec((1, HEAD_DIM), lambda b, h: (0, 0))
    if with_lat:
        tq = _tile((ATTN_TQ, 128), seq)
        t_spec = pl.BlockSpec((seq, LANES), lambda b, h: (0, 0))
        specs = [pl.BlockSpec((seq, gw), lambda b, h: (b, q_col + h)),
                 pl.BlockSpec((seq, HEAD_DIM), lambda b, h: (b, k_col + h)),
                 pl.BlockSpec((seq, HEAD_DIM), lambda b, h: (b, v_col + h)),
                 kc_spec, vc_spec, g_spec, g_spec, t_spec, t_spec, t_spec]
        ins = [q_arr, k_arr, v_arr, k_arr, v_arr, *gains, *tabs]
        kern = functools.partial(_gattn_lat_kernel, group=group, tq=tq)
        scratch = [pltpu.VMEM((seq + ctx, HEAD_DIM), BF16), pltpu.VMEM((seq + ctx, 2 * HEAD_DIM), BF16),
                   pltpu.VMEM((group, tq, seq + ctx), F32), pltpu.VMEM((group, tq, seq + ctx), F32)]
        o_spec = pl.BlockSpec((seq, gw), lambda b, h: (b, h))
    else:
        rms = gains is not None
        specs = [pl.BlockSpec((ctx, gw), lambda b, h: (nlat_c + b, q_col + h)), kc_spec, vc_spec]
        ins = [q_arr, k_arr, v_arr]
        if rms:
            specs += [g_spec, g_spec]
            ins += list(gains)
        kern = functools.partial(_gattn_ctx_kernel, group=group, rms=rms)
        scratch = []
        o_spec = pl.BlockSpec((ctx, gw), lambda b, h: (b, h))
    return pl.pallas_call(
        kern, grid=(nb, n_kv), in_specs=specs, out_specs=o_spec,
        out_shape=jax.ShapeDtypeStruct((out_rows, n_kv * gw), BF16),
        scratch_shapes=scratch,
        compiler_params=_cp("parallel", "parallel"), name=name,
    )(*ins)


def _mla_q(qn, qr, hh, first_half):
    own = first_half if hh == 0 else jnp.logical_not(first_half)
    return jnp.concatenate([qn, jnp.where(own, qr, 0.0).astype(BF16)], axis=1)


def _mla_lat_kernel(qn_ref, qr_ref, kvl_ref, krl_ref, kvc_ref, krc_ref, cos_ref, sa_ref, sb_ref,
                    o_ref, kf_ref, vf_ref, s0_ref, s1_ref, *, tq):
    nl = kvl_ref.shape[0]
    kr = _rope(krl_ref[...].astype(F32), (cos_ref[...], sa_ref[...], sb_ref[...]), MLA_ROPE // 4).astype(BF16)
    for hh in range(2):
        c0 = hh * 2 * HEAD_DIM
        kf_ref[hh, :nl, :MLA_NOPE] = kvl_ref[:, c0:c0 + MLA_NOPE]
        kf_ref[hh, :nl, MLA_NOPE:] = kr
        kf_ref[hh, nl:, :MLA_NOPE] = kvc_ref[:, c0:c0 + MLA_NOPE]
        kf_ref[hh, nl:, MLA_NOPE:] = krc_ref[...]
        _stage_values(vf_ref.at[hh], kvl_ref[:, c0 + MLA_NOPE:c0 + 2 * HEAD_DIM],
                      kvc_ref[:, c0 + MLA_NOPE:c0 + 2 * HEAD_DIM])
    first_half = lax.broadcasted_iota(jnp.int32, (tq, LANES), 1) < MLA_ROPE

    s_refs = (s0_ref, s1_ref)

    def qk(i, slot):
        rows = _chunk_rows(i, tq)
        qr = _rope(qr_ref[rows, :].astype(F32), (cos_ref[rows, :], sa_ref[rows, :], sb_ref[rows, :]), MLA_ROPE // 4)
        for hh in range(2):
            q = _mla_q(qn_ref[rows, hh * MLA_NOPE:(hh + 1) * MLA_NOPE], qr, hh, first_half)
            s_refs[slot][hh] = _dot_nt(q, kf_ref[hh])

    def sm_pv(i, slot):
        rows = _chunk_rows(i, tq)
        for hh in range(2):
            o = _softmax_pv_ones(s_refs[slot][hh], vf_ref[hh])
            o_ref[rows, hh * HEAD_DIM:(hh + 1) * HEAD_DIM] = o.astype(o_ref.dtype)

    _pipelined_chunks(nl // tq, qk, sm_pv)


def _mla_ctx_kernel(qn_ref, qr_ref, kvc_ref, krc_ref, o_ref):
    qr = qr_ref[...].astype(F32)
    first_half = lax.broadcasted_iota(jnp.int32, qr.shape, 1) < MLA_ROPE
    for hh in range(2):
        c0 = hh * 2 * HEAD_DIM
        q = _mla_q(qn_ref[:, hh * MLA_NOPE:(hh + 1) * MLA_NOPE], qr, hh, first_half)
        k = jnp.concatenate([kvc_ref[:, c0:c0 + MLA_NOPE], krc_ref[...]], axis=1)
        v = kvc_ref[:, c0 + MLA_NOPE:c0 + 2 * HEAD_DIM]
        o_ref[:, hh * HEAD_DIM:(hh + 1) * HEAD_DIM] = _softmax_pv(_dot_nt(q, k), v).astype(o_ref.dtype)


def _mla(p_arr, qn_col, qr_col, kv_arr, kr_arr, *, heads, nb, seq, ctx, with_lat, out_rows, name, tabs=None):
    nlat_c = nb * seq // ctx
    qrow = (lambda b: b) if with_lat else (lambda b: nlat_c + b)
    qlen = seq if with_lat else ctx
    specs = [pl.BlockSpec((qlen, 2 * MLA_NOPE), lambda b, h: (qrow(b), qn_col + h)),
             pl.BlockSpec((qlen, 2 * MLA_ROPE), lambda b, h: (qrow(b), qr_col + h))]
    ins = [p_arr, p_arr]
    if with_lat:
        specs += [pl.BlockSpec((seq, 4 * HEAD_DIM), lambda b, h: (b, h)),
                  pl.BlockSpec((seq, LANES), lambda b, h: (b, 0))]
        ins += [kv_arr, kr_arr]
    specs += [pl.BlockSpec((ctx, 4 * HEAD_DIM), lambda b, h: (nlat_c + b, h)),
              pl.BlockSpec((ctx, LANES), lambda b, h: (nlat_c + b, 0))]
    ins += [kv_arr, kr_arr]
    if with_lat:
        tq = _tile((ATTN_TQ, 128), seq)
        specs += [pl.BlockSpec((seq, LANES), lambda b, h: (0, 0))] * 3
        ins += list(tabs)
        kern = functools.partial(_mla_lat_kernel, tq=tq)
        scratch = [pltpu.VMEM((2, seq + ctx, 2 * HEAD_DIM), BF16), pltpu.VMEM((2, seq + ctx, 2 * HEAD_DIM), BF16),
                   pltpu.VMEM((2, tq, seq + ctx), F32), pltpu.VMEM((2, tq, seq + ctx), F32)]
    else:
        kern, scratch = _mla_ctx_kernel, []
    return pl.pallas_call(
        kern, grid=(nb, heads // 2), in_specs=specs,
        out_specs=pl.BlockSpec((qlen, 2 * HEAD_DIM), lambda b, h: (b, h)),
        out_shape=jax.ShapeDtypeStruct((out_rows, heads * HEAD_DIM), BF16),
        scratch_shapes=scratch,
        compiler_params=_cp("parallel", "parallel"), name=name,
    )(*ins)


def _na_fill_bias(w_ref, bias_ref):
    cq = lax.broadcasted_iota(jnp.int32, (GRID_W, LANES), 0)
    lane = lax.broadcasted_iota(jnp.int32, (GRID_W, LANES), 1)
    ck = lane % GRID_W
    cs = jnp.clip(cq - NA_KC // 2, 0, GRID_W - NA_KC)
    col_ok = (ck >= cs) & (ck < cs + NA_KC)
    even = lane < GRID_W
    ninf = jnp.full((GRID_W, LANES), NEG_INF, F32)
    w = w_ref[...] * LOG2E
    toe = {}

    def toeplitz(a, odd):
        if (a, odd) not in toe:
            wa = jnp.broadcast_to(w[a:a + 1, :], (GRID_W, LANES))
            toe[(a, odd)] = pltpu.roll(wa, GRID_W if odd else 0, 1, stride=1, stride_axis=0)
        return toe[(a, odd)]

    for kind, r0_rel in enumerate((0, NA_KR // 2, NA_WROWS - NA_QROWS)):
        for rq in range(NA_QROWS):
            rs_rel = (0, rq, NA_WROWS - NA_KR)[kind]
            for pair in range(NA_WROWS // 2):
                halves = []
                for odd in (0, 1):
                    rk = 2 * pair + odd
                    if rs_rel <= rk < rs_rel + NA_KR:
                        halves.append(toeplitz(rk - r0_rel - rq + NA_KR - 1, odd))
                    else:
                        halves.append(ninf)
                blk = jnp.where(col_ok, jnp.where(even, halves[0], halves[1]), NEG_INF)
                bias_ref[kind, rq * GRID_W:(rq + 1) * GRID_W, pair * LANES:(pair + 1) * LANES] = blk


def _na_kernel(q_ref, k_ref, v_ref, kc_ref, vc_ref, w_ref, o_ref, bias_ref, *, nblk):
    qt = NA_QROWS * GRID_W
    wt = NA_WROWS * GRID_W

    @pl.when(pl.program_id(1) == 0)
    def _():
        _na_fill_bias(w_ref, bias_ref)

    kc, vc = kc_ref[...], vc_ref[...]
    win = lambda rb: slice(min(max(rb - 1, 0), nblk - 3) * qt, min(max(rb - 1, 0), nblk - 3) * qt + wt)

    def scores(rb):
        q = q_ref[rb * qt:(rb + 1) * qt, :]
        kind = 0 if rb == 0 else (2 if rb == nblk - 1 else 1)
        return _dot_nt(q, k_ref[win(rb), :]) + bias_ref[kind], _dot_nt(q, kc)

    nxt = scores(0)
    for rb in range(nblk):
        s_lat, s_ctx = nxt
        if rb + 1 < nblk:
            nxt = scores(rb + 1)
        m = jnp.maximum(jnp.max(s_lat, -1, keepdims=True), jnp.max(s_ctx, -1, keepdims=True))
        e_lat, e_ctx = jnp.exp2(s_lat - m), jnp.exp2(s_ctx - m)
        den = jnp.sum(e_lat, -1, keepdims=True) + jnp.sum(e_ctx, -1, keepdims=True)
        o = _dot(e_lat.astype(BF16), v_ref[win(rb), :]) + _dot(e_ctx.astype(BF16), vc)
        o_ref[rb * qt:(rb + 1) * qt, :] = (o * (1.0 / den)).astype(o_ref.dtype)


def _na(p_arr, q_col, rpb_w, *, heads, nb, seq, ctx, out_rows):
    nlat_c = nb * seq // ctx
    nblk = seq // (NA_QROWS * GRID_W)
    qt, wt = NA_QROWS * GRID_W, NA_WROWS * GRID_W
    lat = lambda off: pl.BlockSpec((seq, HEAD_DIM), lambda h, b: (b, q_col + off + h))
    cx = lambda off: pl.BlockSpec((ctx, HEAD_DIM), lambda h, b: (nlat_c + b, q_col + off + h))
    return pl.pallas_call(
        functools.partial(_na_kernel, nblk=nblk),
        grid=(heads, nb),
        in_specs=[lat(0), lat(heads), lat(2 * heads), cx(heads), cx(2 * heads),
                  pl.BlockSpec((None,) + rpb_w.shape[1:], lambda h, b: (h, 0, 0))],
        out_specs=pl.BlockSpec((seq, HEAD_DIM), lambda h, b: (b, h)),
        out_shape=jax.ShapeDtypeStruct((out_rows, heads * HEAD_DIM), BF16),
        scratch_shapes=[pltpu.VMEM((3, qt, wt), F32)],
        compiler_params=_cp("parallel", "arbitrary"), name="na_lat",
    )(p_arr, p_arr, p_arr, p_arr, p_arr, rpb_w)


def _merge_kernel(ya_ref, yb_ref, yc_ref, w_ref, ga_ref, gb_ref, gc_ref, o_ref):
    acc = None
    for i, (y_ref, g_ref) in enumerate(((ya_ref, ga_ref), (yb_ref, gb_ref), (yc_ref, gc_ref))):
        t = jax.nn.sigmoid(g_ref[...].astype(F32)) * _dot(y_ref[...], w_ref[i])
        acc = t if acc is None else acc + t
    o_ref[...] = acc.astype(o_ref.dtype)


def _merge(ya, yb, yc, w_branch, p_arr, gate_col, *, rows):
    d = ya.shape[1]
    n = w_branch.shape[2]
    tm = _tile((512, 256, 128), rows)
    tn = _tile((512, 256, 128), n)
    y_spec = pl.BlockSpec((tm, d), lambda i, j: (i, 0))
    g_spec = lambda b: pl.BlockSpec((tm, tn), lambda i, j: (i, (gate_col + b * n) // tn + j))
    return pl.pallas_call(
        _merge_kernel, grid=(rows // tm, n // tn),
        in_specs=[y_spec, y_spec, y_spec, pl.BlockSpec((3, d, tn), lambda i, j: (0, 0, j)),
                  g_spec(0), g_spec(1), g_spec(2)],
        out_specs=pl.BlockSpec((tm, tn), lambda i, j: (i, j)),
        out_shape=jax.ShapeDtypeStruct((rows, n), BF16),
        compiler_params=_cp("parallel", "parallel"), name="merge",
    )(ya, yb, yc, w_branch, p_arr, p_arr, p_arr)


def _post_ln_epilogue(y, x, m, gate_row, g, b, alpha):
    z = alpha * x + m[gate_row:gate_row + 1] * y
    return _ln(z, POST_EPS) * g + b


def _out_ln_kernel(a_ref, w_ref, x_ref, m_ref, g_ref, b_ref, xo_ref, h_ref, *, alpha):
    m = m_ref[...]
    xn = _post_ln_epilogue(_dot(a_ref[...], w_ref[...]), x_ref[...], m, 2, g_ref[...], b_ref[...], alpha)
    xo_ref[...] = xn
    h_ref[...] = (_ln(xn, ADA_EPS) * (1.0 + m[4:5]) + m[3:4]).astype(BF16)


def _out_ln(a, w_out, x, mods, ln_g, ln_b, *, rows, seq, nb, alpha):
    d = x.shape[1]
    tm = _tile((256, 128), rows, seq)
    row = pl.BlockSpec((tm, d), lambda i: (i, 0))
    vec = pl.BlockSpec((1, d), lambda i: (0, 0))
    return pl.pallas_call(
        functools.partial(_out_ln_kernel, alpha=alpha), grid=(rows // tm,),
        in_specs=[row, pl.BlockSpec((d, d), lambda i: (0, 0)), row,
                  pl.BlockSpec((None, 6, d), lambda i: (jnp.minimum(i * tm // seq, nb), 0, 0)), vec, vec],
        out_specs=[row, row],
        out_shape=[jax.ShapeDtypeStruct((rows, d), F32), jax.ShapeDtypeStruct((rows, d), BF16)],
        compiler_params=_cp("parallel"), name="out_ln",
    )(a, w_out, x, mods, ln_g, ln_b)


def _conv_kernel(g_ref, v_ref, gp_ref, gn_ref, cw_ref, cb_ref, o_ref, buf_ref, *, tm, seq, ctx, nlat):
    base = pl.program_id(0) * tm
    in_lat = base < nlat
    pos0 = jnp.where(in_lat, base % seq, (base - nlat) % ctx)
    seq_len = jnp.where(in_lat, seq, ctx)
    g = g_ref[...].astype(F32)
    prev_row = gp_ref[SUBLANES - 1:SUBLANES, :].astype(F32)
    next_row = gn_ref[0:1, :].astype(F32)
    buf_ref[SUBLANES:SUBLANES + tm, :] = g
    buf_ref[SUBLANES - 1:SUBLANES, :] = jnp.where(pos0 == 0, 0.0, prev_row)
    buf_ref[SUBLANES + tm:SUBLANES + tm + 1, :] = jnp.where(pos0 + tm == seq_len, 0.0, next_row)
    cw = cw_ref[...]
    gc = (buf_ref[SUBLANES - 1:SUBLANES - 1 + tm, :] * cw[0:1] + g * cw[1:2]
          + buf_ref[SUBLANES + 1:SUBLANES + 1 + tm, :] * cw[2:3] + cb_ref[...])
    o_ref[...] = (gc * jax.nn.sigmoid(gc) * v_ref[...].astype(F32)).astype(o_ref.dtype)


def _conv_gate(u, conv_w, conv_b, *, rows, seq, ctx, nlat):
    dff = conv_w.shape[1]
    tm = _tile((512, 256, 128), rows, ctx)
    tf = _tile((1408, 1024, 512, 256, 128), dff)
    nf = dff // tf
    hb = tm // SUBLANES
    last_hb = rows // SUBLANES - 1
    return pl.pallas_call(
        functools.partial(_conv_kernel, tm=tm, seq=seq, ctx=ctx, nlat=nlat),
        grid=(rows // tm, nf),
        in_specs=[pl.BlockSpec((tm, tf), lambda i, j: (i, j)),
                  pl.BlockSpec((tm, tf), lambda i, j: (i, nf + j)),
                  pl.BlockSpec((SUBLANES, tf), lambda i, j: (jnp.maximum(i * hb - 1, 0), j)),
                  pl.BlockSpec((SUBLANES, tf), lambda i, j: (jnp.minimum((i + 1) * hb, last_hb), j)),
                  pl.BlockSpec((3, tf), lambda i, j: (0, j)),
                  pl.BlockSpec((1, tf), lambda i, j: (0, j))],
        out_specs=pl.BlockSpec((tm, tf), lambda i, j: (i, j)),
        out_shape=jax.ShapeDtypeStruct((rows, dff), BF16),
        scratch_shapes=[pltpu.VMEM((tm + 2 * SUBLANES, tf), F32)],
        compiler_params=_cp("parallel", "parallel"), name="conv_gate",
    )(u, u, u, u, conv_w, conv_b)


def _down_ln_kernel(*refs, alpha, nk, with_next):
    if with_next:
        a_ref, w_ref, x_ref, m_ref, g_ref, b_ref, mn_ref, xo_ref, h_ref, acc_ref = refs
    else:
        a_ref, w_ref, x_ref, m_ref, g_ref, b_ref, xo_ref, acc_ref = refs
    k = pl.program_id(1)

    @pl.when(k == 0)
    def _():
        acc_ref[...] = jnp.zeros_like(acc_ref)

    acc_ref[...] += _dot(a_ref[...], w_ref[...])

    @pl.when(k == nk - 1)
    def _():
        xn = _post_ln_epilogue(acc_ref[...], x_ref[...], m_ref[...], 5, g_ref[...], b_ref[...], alpha)
        xo_ref[...] = xn
        if with_next:
            mn = mn_ref[...]
            h_ref[...] = (_ln(xn, ADA_EPS) * (1.0 + mn[1:2]) + mn[0:1]).astype(BF16)


def _down_ln(a, w_down, x, mods, ln_g, ln_b, mods_next, *, rows, seq, nb, alpha):
    d = x.shape[1]
    dff = a.shape[1]
    tm = _tile((512, 256, 128), rows, seq)
    nk = next(c for c in (4, 2, 1, 8, 11, 22, 44) if dff % c == 0 and (dff // c) % LANES == 0)
    tk = dff // nk
    with_next = mods_next is not None
    row = pl.BlockSpec((tm, d), lambda i, k: (i, 0))
    vec = pl.BlockSpec((1, d), lambda i, k: (0, 0))
    mod = pl.BlockSpec((None, 6, d), lambda i, k: (jnp.minimum(i * tm // seq, nb), 0, 0))
    specs = [pl.BlockSpec((tm, tk), lambda i, k: (i, k)), pl.BlockSpec((tk, d), lambda i, k: (k, 0)),
             row, mod, vec, vec]
    ins = [a, w_down, x, mods, ln_g, ln_b]
    out_specs = [row]
    out_shape = [jax.ShapeDtypeStruct((rows, d), F32)]
    if with_next:
        specs.append(mod)
        ins.append(mods_next)
        out_specs.append(row)
        out_shape.append(jax.ShapeDtypeStruct((rows, d), BF16))
    res = pl.pallas_call(
        functools.partial(_down_ln_kernel, alpha=alpha, nk=nk, with_next=with_next),
        grid=(rows // tm, nk), in_specs=specs, out_specs=out_specs, out_shape=out_shape,
        scratch_shapes=[pltpu.VMEM((tm, d), F32)],
        compiler_params=_cp("parallel", "arbitrary"), name="down_ln",
    )(*ins)
    return (res[0], res[1]) if with_next else (res[0], None)


def _rope_tables(seq, dims, half):
    t = jnp.arange(seq)
    rows, cols = (t // GRID_W).astype(F32), (t % GRID_W).astype(F32)
    j = jnp.arange(LANES)
    freqs = ROPE_THETA ** (-jnp.arange(half, dtype=F32) / half)
    f = freqs[j % half]
    use_row = (j % dims) < dims // 2
    ang = jnp.where(use_row[None, :], rows[:, None], cols[:, None]) * f[None, :]
    cos, sin = jnp.cos(ang), jnp.sin(ang)
    upper = (j % (2 * half)) < half
    return cos, jnp.where(upper[None, :], -sin, 0.0), jnp.where(upper[None, :], 0.0, sin)


def _rpb_rows(rpb):
    h, nr, nc = rpb.shape
    w = jnp.pad(rpb.astype(F32), ((0, 0), (0, 2 * SUBLANES - nr), (0, LANES - nc)))
    return jnp.roll(w, -(NA_KC - 1), axis=2)


def kernel(x, c, ctx, c_ctx, w_ada, b_ada, w_in, mla_kv_norm, w_mla_ukv, gqa_q_norm, gqa_k_norm, na_rpb,
           w_branch, w_out, ln_a_g, ln_a_b, w_up, conv_w, conv_b, w_down, ln_f_g, ln_f_b):
    nb, seq, d = x.shape
    lctx = ctx.shape[1]
    depth = w_in.shape[0]
    heads = d // HEAD_DIM
    gkv = heads // 4
    group = heads // gkv
    kvr = mla_kv_norm.shape[1]
    nlat, nctx = nb * seq, nb * lctx
    t_all = nlat + nctx
    alpha = float((2 * depth) ** 0.25)
    mla_scale = float((MLA_NOPE + MLA_ROPE) ** -0.5) * LOG2E
    hd_scale = float(HEAD_DIM ** -0.5) * LOG2E
    assert seq % (NA_QROWS * GRID_W) == 0 and seq // (NA_QROWS * GRID_W) >= 3
    assert heads % 4 == 0 and seq % lctx == 0 and nctx <= seq

    o_ckv = heads * (MLA_NOPE + MLA_ROPE)
    o_kr = o_ckv + kvr
    o_na = o_kr + MLA_ROPE
    o_gq = o_na + 3 * heads * HEAD_DIM
    o_gkv = o_gq + heads * HEAD_DIM
    o_gate = o_gkv + 2 * gkv * HEAD_DIM

    pad = (-(nb + 1)) % SUBLANES
    c_all = jnp.concatenate([c, c_ctx[None, :], jnp.zeros((pad, d), F32)], 0)
    mods = _ada(c_all, w_ada, b_ada).reshape(depth, nb + 1 + pad, 6, d)

    tm_proj = _tile((1024, 512, 256, 128), seq, nctx)
    rope_g = _rope_tables(seq, HEAD_DIM, HEAD_DIM // 4)
    rope_m = _rope_tables(seq, MLA_ROPE, MLA_ROPE // 4)

    xs = jnp.concatenate([x.reshape(nlat, d), ctx.reshape(nctx, d)], 0)
    h = _ln_mod(xs, mods[0], seq, nb)

    c_gq, c_naq, c_gate = 0, heads, 4 * heads
    c_mqn = c_gate + 3 * heads
    c_gv = c_mqn + heads
    c_gk = c_gv + gkv
    c_mqr = c_gk + gkv
    ones = lambda nblk: jnp.ones((nblk * HEAD_DIM,), F32)
    cs_proj = jnp.concatenate([ones(heads), ones(heads) * hd_scale, ones(5 * heads), ones(heads) * mla_scale,
                               ones(2 * gkv), ones(heads // 2) * mla_scale])[None, :]

    for l in range(depth):
        last = l == depth - 1
        rows = nlat if last else t_all
        wl = w_in[l]
        mq = wl[:, :o_ckv].reshape(d, heads, MLA_NOPE + MLA_ROPE)
        gkvw = wl[:, o_gkv:o_gate].reshape(d, 2, gkv * HEAD_DIM)
        w_proj = jnp.concatenate(
            [wl[:, o_gq:o_gkv], wl[:, o_na:o_gq], wl[:, o_gate:], mq[:, :, :MLA_NOPE].reshape(d, heads * MLA_NOPE),
             gkvw[:, 1], gkvw[:, 0], mq[:, :, MLA_NOPE:].reshape(d, heads * MLA_ROPE)], 1).astype(BF16)
        w_kr2 = jnp.concatenate([wl[:, o_kr:o_na], wl[:, o_kr:o_na]], 1).astype(BF16)

        p = _mm(h, w_proj, tm=tm_proj, cs=cs_proj, name="proj")
        p_ckv, p_kr = _proj_ckv(h, wl[:, o_ckv:o_kr].astype(BF16), mla_kv_norm[l][None, :], w_kr2, tm=tm_proj)
        p_kv = _mm(p_ckv, w_mla_ukv[l].astype(BF16), tm=tm_proj, name="mla_ukv")

        dims = dict(nb=nb, seq=seq, ctx=lctx)
        gains = ((gqa_q_norm[l] * hd_scale)[None, :], gqa_k_norm[l][None, :])
        ya = _mla(p, c_mqn // 2, c_mqr, p_kv, p_kr, heads=heads, with_lat=True, out_rows=rows, name="mla_lat",
                  tabs=rope_m, **dims)
        yb = _na(p, c_naq, _rpb_rows(na_rpb[l]), heads=heads, out_rows=rows, **dims)
        yc = _gattn(p, c_gq // group, p, c_gk, p, c_gv, group=group, n_kv=gkv, with_lat=True, out_rows=rows,
                    name="gqa_lat", gains=gains, tabs=rope_g, **dims)
        if not last:
            ya_c = _mla(p, c_mqn // 2, c_mqr, p_kv, p_kr, heads=heads, with_lat=False, out_rows=nctx,
                        name="mla_ctx", **dims)
            yb_c = _gattn(p, c_naq, p, c_naq + heads, p, c_naq + 2 * heads, group=1, n_kv=heads, with_lat=False,
                          out_rows=nctx, name="na_ctx", **dims)
            yc_c = _gattn(p, c_gq // group, p, c_gk, p, c_gv, group=group, n_kv=gkv, with_lat=False,
                          out_rows=nctx, name="gqa_ctx", gains=gains, **dims)
            ya = lax.dynamic_update_slice(ya, ya_c, (nlat, 0))
            yb = lax.dynamic_update_slice(yb, yb_c, (nlat, 0))
            yc = lax.dynamic_update_slice(yc, yc_c, (nlat, 0))

        merged = _merge(ya, yb, yc, w_branch[l].astype(BF16), p, c_gate * HEAD_DIM, rows=rows)
        xs, h2 = _out_ln(merged, w_out[l].astype(BF16), xs, mods[l], ln_a_g[l][None, :], ln_a_b[l][None, :],
                         rows=rows, seq=seq, nb=nb, alpha=alpha)
        u = _mm(h2, w_up[l].astype(BF16), rows=rows, name="ffn_up")
        hid = _conv_gate(u, conv_w[l], conv_b[l][None, :], rows=rows, seq=seq, ctx=lctx, nlat=nlat)
        xs, h = _down_ln(hid, w_down[l].astype(BF16), xs, mods[l], ln_f_g[l][None, :], ln_f_b[l][None, :],
                         None if last else mods[l + 1], rows=rows, seq=seq, nb=nb, alpha=alpha)
    return xs[:nlat].reshape(nb, seq, d)
```

```python
import functools
import math

import jax
import jax.numpy as jnp
from jax import lax
from jax.experimental import pallas as pl
from jax.experimental.pallas import tpu as pltpu

F32 = jnp.float32
BF16 = jnp.bfloat16

GRID_W = 64
HEAD_DIM = 128
MLA_NOPE = 128
MLA_ROPE = 64
NA_KR = 8
NA_KC = 16
NA_QROWS = 4
NA_WROWS = NA_QROWS + NA_KR
ROPE_THETA = 10000.0
ADA_EPS = 1e-6
POST_EPS = 1e-5
RMS_EPS = 1e-6
NEG_INF = -1e30
LOG2E = math.log2(math.e)
LANES = 128
SUBLANES = 8
VMEM_LIMIT = 56 * 2**20
ATTN_TQ = 256


def _cp(*sem):
    return pltpu.CompilerParams(dimension_semantics=sem, vmem_limit_bytes=VMEM_LIMIT)


def _tile(cands, *ns):
    for c in cands:
        if all(n % c == 0 for n in ns):
            return c
    raise ValueError(f"no tile in {cands} divides {ns}")


def _ln(x, eps):
    mu = jnp.mean(x, -1, keepdims=True)
    xc = x - mu
    var = jnp.mean(xc * xc, -1, keepdims=True)
    return xc * lax.rsqrt(var + eps)


def _rms(x):
    return x * lax.rsqrt(jnp.mean(x * x, -1, keepdims=True) + RMS_EPS)


def _rope(x, tabs, shift):
    cos, sa, sb = tabs
    return x * cos + pltpu.roll(x, LANES - shift, 1) * sa + pltpu.roll(x, shift, 1) * sb


def _dot(a, b):
    return jnp.dot(a, b, preferred_element_type=F32)


def _dot_nt(a, b):
    return lax.dot_general(a, b, (((1,), (1,)), ((), ())), preferred_element_type=F32)


def _ada_kernel(c_ref, w_ref, b_ref, o_ref):
    c = c_ref[...]
    a = (c * jax.nn.sigmoid(c)).astype(BF16)
    o_ref[...] = _dot(a, w_ref[...].astype(BF16)) + b_ref[...]


def _ada(c_all, w_ada, b_ada):
    depth, d, n = w_ada.shape
    r = c_all.shape[0]
    tn = _tile((1024, 512, 256, 128), n)
    return pl.pallas_call(
        _ada_kernel,
        grid=(depth, n // tn),
        in_specs=[pl.BlockSpec((r, d), lambda l, j: (0, 0)),
                  pl.BlockSpec((None, d, tn), lambda l, j: (l, 0, j)),
                  pl.BlockSpec((None, 1, tn), lambda l, j: (l, 0, j))],
        out_specs=pl.BlockSpec((None, r, tn), lambda l, j: (l, 0, j)),
        out_shape=jax.ShapeDtypeStruct((depth, r, n), F32),
        compiler_params=_cp("parallel", "parallel"),
        name="ada",
    )(c_all, w_ada, b_ada.reshape(depth, 1, n))


def _ln_mod_kernel(x_ref, c_ref, m_ref, xs_ref, h_ref, *, nlat_tiles):
    m = m_ref[...]

    def emit(src_ref):
        x = src_ref[...]
        xs_ref[...] = x
        h_ref[...] = (_ln(x, ADA_EPS) * (1.0 + m[1:2]) + m[0:1]).astype(BF16)

    pl.when(pl.program_id(0) < nlat_tiles)(lambda: emit(x_ref))
    pl.when(pl.program_id(0) >= nlat_tiles)(lambda: emit(c_ref))


def _ln_mod(x, ctx, mods, seq, nb):
    nlat, d = x.shape
    nctx = ctx.shape[0]
    tm = _tile((512, 256, 128), seq, nctx)
    nl = nlat // tm
    row = pl.BlockSpec((tm, d), lambda i: (i, 0))
    return pl.pallas_call(
        functools.partial(_ln_mod_kernel, nlat_tiles=nl),
        grid=((nlat + nctx) // tm,),
        in_specs=[pl.BlockSpec((tm, d), lambda i: (jnp.minimum(i, nl - 1), 0)),
                  pl.BlockSpec((tm, d), lambda i: (jnp.maximum(i - nl, 0), 0)),
                  pl.BlockSpec((None, 6, d), lambda i: (jnp.minimum(i * tm // seq, nb), 0, 0))],
        out_specs=[row, row],
        out_shape=[jax.ShapeDtypeStruct((nlat + nctx, d), F32), jax.ShapeDtypeStruct((nlat + nctx, d), BF16)],
        compiler_params=_cp("arbitrary"),
        name="ln_mod",
    )(x, ctx, mods)


def _mm_kernel(a_ref, w_ref, o_ref):
    o_ref[...] = _dot(a_ref[...], w_ref[...]).astype(o_ref.dtype)


def _mm_scale_kernel(a_ref, w_ref, cs_ref, o_ref):
    o_ref[...] = (_dot(a_ref[...], w_ref[...]) * cs_ref[...]).astype(o_ref.dtype)


def _mm(a, w, *, rows=None, tm=None, tn=None, cs=None, name="mm"):
    t, k = a.shape
    n = w.shape[1]
    rows = t if rows is None else rows
    tm = tm or _tile((1024, 512, 256, 128), rows)
    tn = tn or _tile((1024, 512, 256, 128), n)
    a_spec = pl.BlockSpec((tm, k), lambda i, j: (i, 0))
    w_spec = pl.BlockSpec((k, tn), lambda i, j: (0, j))
    if cs is not None:
        kern, ins, specs = _mm_scale_kernel, (a, w, cs), [a_spec, w_spec, pl.BlockSpec((1, tn), lambda i, j: (0, j))]
    else:
        kern, ins, specs = _mm_kernel, (a, w), [a_spec, w_spec]
    return pl.pallas_call(
        kern, grid=(rows // tm, n // tn), in_specs=specs,
        out_specs=pl.BlockSpec((tm, tn), lambda i, j: (i, j)),
        out_shape=jax.ShapeDtypeStruct((rows, n), BF16),
        compiler_params=_cp("parallel", "parallel"), name=name,
    )(*ins)


def _ckv_kernel(a_ref, wc_ref, g_ref, wr_ref, oc_ref, or_ref):
    a = a_ref[...]
    oc_ref[...] = (_rms(_dot(a, wc_ref[...])) * g_ref[...]).astype(oc_ref.dtype)
    or_ref[...] = _dot(a, wr_ref[...]).astype(or_ref.dtype)


def _proj_ckv(a, w_ckv, gain, w_kr2, *, tm):
    t, k = a.shape
    r = w_ckv.shape[1]
    full = lambda shape: pl.BlockSpec(shape, lambda i: (0, 0))
    return pl.pallas_call(
        _ckv_kernel, grid=(t // tm,),
        in_specs=[pl.BlockSpec((tm, k), lambda i: (i, 0)), full((k, r)), full((1, r)), full((k, LANES))],
        out_specs=[pl.BlockSpec((tm, r), lambda i: (i, 0)), pl.BlockSpec((tm, LANES), lambda i: (i, 0))],
        out_shape=[jax.ShapeDtypeStruct((t, r), BF16), jax.ShapeDtypeStruct((t, LANES), BF16)],
        compiler_params=_cp("parallel"), name="proj_ckv",
    )(a, w_ckv, gain, w_kr2)


def _softmax_pv(s, v):
    e = jnp.exp2(s - jnp.max(s, -1, keepdims=True))
    den = jnp.sum(e, -1, keepdims=True)
    return _dot(e.astype(BF16), v) * (1.0 / den)


def _softmax_pv_ones(s, v_ext):
    e = jnp.exp2(s - jnp.max(s, -1, keepdims=True)).astype(BF16)
    o = _dot(e, v_ext)
    return o[:, :HEAD_DIM] * (1.0 / o[:, HEAD_DIM:HEAD_DIM + 1])


def _stage_values(vf_ref, v_lat, v_ctx):
    nl = v_lat.shape[0]
    vf_ref[:nl, :HEAD_DIM] = v_lat
    vf_ref[nl:, :HEAD_DIM] = v_ctx
    lane = lax.broadcasted_iota(jnp.int32, (vf_ref.shape[0], LANES), 1)
    vf_ref[:, HEAD_DIM:] = jnp.where(lane == 0, 1.0, 0.0).astype(vf_ref.dtype)


def _chunk_rows(i, tq):
    return pl.ds(i * tq, tq) if isinstance(i, int) else pl.ds(pl.multiple_of(i * tq, tq), tq)


def _pipelined_chunks(nq, qk, sm_pv):
    qk(0, 0)
    npairs = (nq - 1) // 2

    def body(j, carry):
        i = 2 * j
        qk(i + 1, 1)
        sm_pv(i, 0)
        qk(i + 2, 0)
        sm_pv(i + 1, 1)
        return carry

    lax.fori_loop(0, npairs, body, 0)
    i = 2 * npairs
    if i + 1 < nq:
        qk(i + 1, 1)
        sm_pv(i, 0)
        sm_pv(i + 1, 1)
    else:
        sm_pv(i, 0)


def _gattn_lat_kernel(q_ref, kl_ref, vl_ref, kc_ref, vc_ref, qg_ref, kg_ref, cos_ref, sa_ref, sb_ref,
                      o_ref, kf_ref, vf_ref, s0_ref, s1_ref, *, group, tq):
    nl = kl_ref.shape[0]

    def prep(x, g_ref, tabs):
        x = _rms(x.astype(F32)) * g_ref[...]
        if tabs is not None:
            x = _rope(x, tabs, HEAD_DIM // 4)
        return x.astype(BF16)

    kf_ref[:nl, :] = prep(kl_ref[...], kg_ref, (cos_ref[...], sa_ref[...], sb_ref[...]))
    kf_ref[nl:, :] = prep(kc_ref[...], kg_ref, None)
    _stage_values(vf_ref, vl_ref[...], vc_ref[...])

    s_refs = (s0_ref, s1_ref)

    def qk(i, slot):
        rows = _chunk_rows(i, tq)
        tabs = (cos_ref[rows, :], sa_ref[rows, :], sb_ref[rows, :])
        for g in range(group):
            q = prep(q_ref[rows, g * HEAD_DIM:(g + 1) * HEAD_DIM], qg_ref, tabs)
            s_refs[slot][g] = _dot_nt(q, kf_ref[...])

    def sm_pv(i, slot):
        rows = _chunk_rows(i, tq)
        for g in range(group):
            o = _softmax_pv_ones(s_refs[slot][g], vf_ref[...])
            o_ref[rows, g * HEAD_DIM:(g + 1) * HEAD_DIM] = o.astype(o_ref.dtype)

    _pipelined_chunks(nl // tq, qk, sm_pv)


def _gattn_ctx_kernel(*refs, group, rms):
    if rms:
        q_ref, kc_ref, vc_ref, qg_ref, kg_ref, o_ref = refs
        norm = lambda x, g_ref: (_rms(x.astype(F32)) * g_ref[...]).astype(BF16)
        k = norm(kc_ref[...], kg_ref)
    else:
        q_ref, kc_ref, vc_ref, o_ref = refs
        k = kc_ref[...]
    for g in range(group):
        sl = slice(g * HEAD_DIM, (g + 1) * HEAD_DIM)
        q = norm(q_ref[:, sl], qg_ref) if rms else q_ref[:, sl]
        o_ref[:, sl] = _softmax_pv(_dot_nt(q, k), vc_ref[...]).astype(o_ref.dtype)


def _gattn(q_arr, q_col, k_arr, k_col, v_arr, v_col, *, group, n_kv, nb, seq, ctx, with_lat, out_rows, name,
           gains=None, tabs=None):
    nlat_c = nb * seq // ctx
    gw = group * HEAD_DIM
    kc_spec = pl.BlockSpec((ctx, HEAD_DIM), lambda b, h: (nlat_c + b, k_col + h))
    vc_spec = pl.BlockSpec((ctx, HEAD_DIM), lambda b, h: (nlat_c + b, v_col + h))
    g_spec = pl.BlockSpec((1, HEAD_DIM), lambda b, h: (0, 0))
    if with_lat:
        tq = _tile((ATTN_TQ, 128), seq)
        t_spec = pl.BlockSpec((seq, LANES), lambda b, h: (0, 0))
        specs = [pl.BlockSpec((seq, gw), lambda b, h: (b, q_col + h)),
                 pl.BlockSpec((seq, HEAD_DIM), lambda b, h: (b, k_col + h)),
                 pl.BlockSpec((seq, HEAD_DIM), lambda b, h: (b, v_col + h)),
                 kc_spec, vc_spec, g_spec, g_spec, t_spec, t_spec, t_spec]
        ins = [q_arr, k_arr, v_arr, k_arr, v_arr, *gains, *tabs]
        kern = functools.partial(_gattn_lat_kernel, group=group, tq=tq)
        scratch = [pltpu.VMEM((seq + ctx, HEAD_DIM), BF16), pltpu.VMEM((seq + ctx, 2 * HEAD_DIM), BF16),
                   pltpu.VMEM((group, tq, seq + ctx), F32), pltpu.VMEM((group, tq, seq + ctx), F32)]
        o_spec = pl.BlockSpec((seq, gw), lambda b, h: (b, h))
    else:
        rms = gains is not None
        specs = [pl.BlockSpec((ctx, gw), lambda b, h: (nlat_c + b, q_col + h)), kc_spec, vc_spec]
        ins = [q_arr, k_arr, v_arr]
        if rms:
            specs += [g_spec, g_spec]
            ins += list(gains)
        kern = functools.partial(_gattn_ctx_kernel, group=group, rms=rms)
        scratch = []
        o_spec = pl.BlockSpec((ctx, gw), lambda b, h: (b, h))
    return pl.pallas_call(
        kern, grid=(nb, n_kv), in_specs=specs, out_specs=o_spec,
        out_shape=jax.ShapeDtypeStruct((out_rows, n_kv * gw), BF16),
        scratch_shapes=scratch,
        compiler_params=_cp("parallel", "parallel"), name=name,
    )(*ins)


def _mla_q(qn, qr, hh, first_half):
    own = first_half if hh == 0 else jnp.logical_not(first_half)
    return jnp.concatenate([qn, jnp.where(own, qr, 0.0).astype(BF16)], axis=1)


def _mla_lat_kernel(qn_ref, qr_ref, kvl_ref, krl_ref, kvc_ref, krc_ref, cos_ref, sa_ref, sb_ref,
                    o_ref, kf_ref, vf_ref, s0_ref, s1_ref, *, tq):
    nl = kvl_ref.shape[0]
    kr = _rope(krl_ref[...].astype(F32), (cos_ref[...], sa_ref[...], sb_ref[...]), MLA_ROPE // 4).astype(BF16)
    for hh in range(2):
        c0 = hh * 2 * HEAD_DIM
        kf_ref[hh, :nl, :MLA_NOPE] = kvl_ref[:, c0:c0 + MLA_NOPE]
        kf_ref[hh, :nl, MLA_NOPE:] = kr
        kf_ref[hh, nl:, :MLA_NOPE] = kvc_ref[:, c0:c0 + MLA_NOPE]
        kf_ref[hh, nl:, MLA_NOPE:] = krc_ref[...]
        _stage_values(vf_ref.at[hh], kvl_ref[:, c0 + MLA_NOPE:c0 + 2 * HEAD_DIM],
                      kvc_ref[:, c0 + MLA_NOPE:c0 + 2 * HEAD_DIM])
    first_half = lax.broadcasted_iota(jnp.int32, (tq, LANES), 1) < MLA_ROPE

    s_refs = (s0_ref, s1_ref)

    def qk(i, slot):
        rows = _chunk_rows(i, tq)
        qr = _rope(qr_ref[rows, :].astype(F32), (cos_ref[rows, :], sa_ref[rows, :], sb_ref[rows, :]), MLA_ROPE // 4)
        for hh in range(2):
            q = _mla_q(qn_ref[rows, hh * MLA_NOPE:(hh + 1) * MLA_NOPE], qr, hh, first_half)
            s_refs[slot][hh] = _dot_nt(q, kf_ref[hh])

    def sm_pv(i, slot):
        rows = _chunk_rows(i, tq)
        for hh in range(2):
            o = _softmax_pv_ones(s_refs[slot][hh], vf_ref[hh])
            o_ref[rows, hh * HEAD_DIM:(hh + 1) * HEAD_DIM] = o.astype(o_ref.dtype)

    _pipelined_chunks(nl // tq, qk, sm_pv)


def _mla_ctx_kernel(qn_ref, qr_ref, kvc_ref, krc_ref, o_ref):
    qr = qr_ref[...].astype(F32)
    first_half = lax.broadcasted_iota(jnp.int32, qr.shape, 1) < MLA_ROPE
    for hh in range(2):
        c0 = hh * 2 * HEAD_DIM
        q = _mla_q(qn_ref[:, hh * MLA_NOPE:(hh + 1) * MLA_NOPE], qr, hh, first_half)
        k = jnp.concatenate([kvc_ref[:, c0:c0 + MLA_NOPE], krc_ref[...]], axis=1)
        v = kvc_ref[:, c0 + MLA_NOPE:c0 + 2 * HEAD_DIM]
        o_ref[:, hh * HEAD_DIM:(hh + 1) * HEAD_DIM] = _softmax_pv(_dot_nt(q, k), v).astype(o_ref.dtype)


def _mla(p_arr, qn_col, qr_col, kv_arr, kr_arr, *, heads, nb, seq, ctx, with_lat, out_rows, name, tabs=None):
    nlat_c = nb * seq // ctx
    qrow = (lambda b: b) if with_lat else (lambda b: nlat_c + b)
    qlen = seq if with_lat else ctx
    specs = [pl.BlockSpec((qlen, 2 * MLA_NOPE), lambda b, h: (qrow(b), qn_col + h)),
             pl.BlockSpec((qlen, 2 * MLA_ROPE), lambda b, h: (qrow(b), qr_col + h))]
    ins = [p_arr, p_arr]
    if with_lat:
        specs += [pl.BlockSpec((seq, 4 * HEAD_DIM), lambda b, h: (b, h)),
                  pl.BlockSpec((seq, LANES), lambda b, h: (b, 0))]
        ins += [kv_arr, kr_arr]
    specs += [pl.BlockSpec((ctx, 4 * HEAD_DIM), lambda b, h: (nlat_c + b, h)),
              pl.BlockSpec((ctx, LANES), lambda b, h: (nlat_c + b, 0))]
    ins += [kv_arr, kr_arr]
    if with_lat:
        tq = _tile((ATTN_TQ, 128), seq)
        specs += [pl.BlockSpec((seq, LANES), lambda b, h: (0, 0))] * 3
        ins += list(tabs)
        kern = functools.partial(_mla_lat_kernel, tq=tq)
        scratch = [pltpu.VMEM((2, seq + ctx, 2 * HEAD_DIM), BF16), pltpu.VMEM((2, seq + ctx, 2 * HEAD_DIM), BF16),
                   pltpu.VMEM((2, tq, seq + ctx), F32), pltpu.VMEM((2, tq, seq + ctx), F32)]
    else:
        kern, scratch = _mla_ctx_kernel, []
    return pl.pallas_call(
        kern, grid=(nb, heads // 2), in_specs=specs,
        out_specs=pl.BlockSpec((qlen, 2 * HEAD_DIM), lambda b, h: (b, h)),
        out_shape=jax.ShapeDtypeStruct((out_rows, heads * HEAD_DIM), BF16),
        scratch_shapes=scratch,
        compiler_params=_cp("parallel", "parallel"), name=name,
    )(*ins)


def _na_fill_bias(w_ref, bias_ref):
    cq = lax.broadcasted_iota(jnp.int32, (GRID_W, LANES), 0)
    lane = lax.broadcasted_iota(jnp.int32, (GRID_W, LANES), 1)
    ck = lane % GRID_W
    cs = jnp.clip(cq - NA_KC // 2, 0, GRID_W - NA_KC)
    col_ok = (ck >= cs) & (ck < cs + NA_KC)
    even = lane < GRID_W
    ninf = jnp.full((GRID_W, LANES), NEG_INF, F32)
    w = w_ref[...] * LOG2E
    toe = {}

    def toeplitz(a, odd):
        if (a, odd) not in toe:
            wa = jnp.broadcast_to(w[a:a + 1, :], (GRID_W, LANES))
            toe[(a, odd)] = pltpu.roll(wa, GRID_W if odd else 0, 1, stride=1, stride_axis=0)
        return toe[(a, odd)]

    for kind, r0_rel in enumerate((0, NA_KR // 2, NA_WROWS - NA_QROWS)):
        for rq in range(NA_QROWS):
            rs_rel = (0, rq, NA_WROWS - NA_KR)[kind]
            for pair in range(NA_WROWS // 2):
                halves = []
                for odd in (0, 1):
                    rk = 2 * pair + odd
                    if rs_rel <= rk < rs_rel + NA_KR:
                        halves.append(toeplitz(rk - r0_rel - rq + NA_KR - 1, odd))
                    else:
                        halves.append(ninf)
                blk = jnp.where(col_ok, jnp.where(even, halves[0], halves[1]), NEG_INF)
                bias_ref[kind, rq * GRID_W:(rq + 1) * GRID_W, pair * LANES:(pair + 1) * LANES] = blk


def _na_kernel(q_ref, k_ref, v_ref, kc_ref, vc_ref, w_ref, o_ref, bias_ref, *, nblk):
    qt = NA_QROWS * GRID_W
    wt = NA_WROWS * GRID_W

    @pl.when(pl.program_id(1) == 0)
    def _():
        _na_fill_bias(w_ref, bias_ref)

    kc, vc = kc_ref[...], vc_ref[...]
    win = lambda rb: slice(min(max(rb - 1, 0), nblk - 3) * qt, min(max(rb - 1, 0), nblk - 3) * qt + wt)

    def scores(rb):
        q = q_ref[rb * qt:(rb + 1) * qt, :]
        kind = 0 if rb == 0 else (2 if rb == nblk - 1 else 1)
        return _dot_nt(q, k_ref[win(rb), :]) + bias_ref[kind], _dot_nt(q, kc)

    nxt = scores(0)
    for rb in range(nblk):
        s_lat, s_ctx = nxt
        if rb + 1 < nblk:
            nxt = scores(rb + 1)
        m = jnp.maximum(jnp.max(s_lat, -1, keepdims=True), jnp.max(s_ctx, -1, keepdims=True))
        e_lat, e_ctx = jnp.exp2(s_lat - m), jnp.exp2(s_ctx - m)
        den = jnp.sum(e_lat, -1, keepdims=True) + jnp.sum(e_ctx, -1, keepdims=True)
        o = _dot(e_lat.astype(BF16), v_ref[win(rb), :]) + _dot(e_ctx.astype(BF16), vc)
        o_ref[rb * qt:(rb + 1) * qt, :] = (o * (1.0 / den)).astype(o_ref.dtype)


def _na(p_arr, q_col, rpb_w, *, heads, nb, seq, ctx, out_rows):
    nlat_c = nb * seq // ctx
    nblk = seq // (NA_QROWS * GRID_W)
    qt, wt = NA_QROWS * GRID_W, NA_WROWS * GRID_W
    lat = lambda off: pl.BlockSpec((seq, HEAD_DIM), lambda h, b: (b, q_col + off + h))
    cx = lambda off: pl.BlockSpec((ctx, HEAD_DIM), lambda h, b: (nlat_c + b, q_col + off + h))
    return pl.pallas_call(
        functools.partial(_na_kernel, nblk=nblk),
        grid=(heads, nb),
        in_specs=[lat(0), lat(heads), lat(2 * heads), cx(heads), cx(2 * heads),
                  pl.BlockSpec((None,) + rpb_w.shape[1:], lambda h, b: (h, 0, 0))],
        out_specs=pl.BlockSpec((seq, HEAD_DIM), lambda h, b: (b, h)),
        out_shape=jax.ShapeDtypeStruct((out_rows, heads * HEAD_DIM), BF16),
        scratch_shapes=[pltpu.VMEM((3, qt, wt), F32)],
        compiler_params=_cp("parallel", "arbitrary"), name="na_lat",
    )(p_arr, p_arr, p_arr, p_arr, p_arr, rpb_w)


def _merge_kernel(ya_ref, yb_ref, yc_ref, w_ref, ga_ref, gb_ref, gc_ref, o_ref):
    acc = None
    for i, (y_ref, g_ref) in enumerate(((ya_ref, ga_ref), (yb_ref, gb_ref), (yc_ref, gc_ref))):
        t = jax.nn.sigmoid(g_ref[...].astype(F32)) * _dot(y_ref[...], w_ref[i])
        acc = t if acc is None else acc + t
    o_ref[...] = acc.astype(o_ref.dtype)


def _merge(ya, yb, yc, w_branch, p_arr, gate_col, *, rows):
    d = ya.shape[1]
    n = w_branch.shape[2]
    tm = _tile((512, 256, 128), rows)
    tn = _tile((512, 256, 128), n)
    y_spec = pl.BlockSpec((tm, d), lambda i, j: (i, 0))
    g_spec = lambda b: pl.BlockSpec((tm, tn), lambda i, j: (i, (gate_col + b * n) // tn + j))
    return pl.pallas_call(
        _merge_kernel, grid=(rows // tm, n // tn),
        in_specs=[y_spec, y_spec, y_spec, pl.BlockSpec((3, d, tn), lambda i, j: (0, 0, j)),
                  g_spec(0), g_spec(1), g_spec(2)],
        out_specs=pl.BlockSpec((tm, tn), lambda i, j: (i, j)),
        out_shape=jax.ShapeDtypeStruct((rows, n), BF16),
        compiler_params=_cp("parallel", "parallel"), name="merge",
    )(ya, yb, yc, w_branch, p_arr, p_arr, p_arr)


def _post_ln_epilogue(y, x, m, gate_row, g, b, alpha):
    z = alpha * x + m[gate_row:gate_row + 1] * y
    return _ln(z, POST_EPS) * g + b


def _out_ln_kernel(a_ref, w_ref, x_ref, m_ref, g_ref, b_ref, xo_ref, h_ref, *, alpha):
    m = m_ref[...]
    xn = _post_ln_epilogue(_dot(a_ref[...], w_ref[...]), x_ref[...], m, 2, g_ref[...], b_ref[...], alpha)
    xo_ref[...] = xn
    h_ref[...] = (_ln(xn, ADA_EPS) * (1.0 + m[4:5]) + m[3:4]).astype(BF16)


def _out_ln(a, w_out, x, mods, ln_g, ln_b, *, rows, seq, nb, alpha):
    d = x.shape[1]
    tm = _tile((256, 128), rows, seq)
    row = pl.BlockSpec((tm, d), lambda i: (i, 0))
    vec = pl.BlockSpec((1, d), lambda i: (0, 0))
    return pl.pallas_call(
        functools.partial(_out_ln_kernel, alpha=alpha), grid=(rows // tm,),
        in_specs=[row, pl.BlockSpec((d, d), lambda i: (0, 0)), row,
                  pl.BlockSpec((None, 6, d), lambda i: (jnp.minimum(i * tm // seq, nb), 0, 0)), vec, vec],
        out_specs=[row, row],
        out_shape=[jax.ShapeDtypeStruct((rows, d), F32), jax.ShapeDtypeStruct((rows, d), BF16)],
        compiler_params=_cp("parallel"), name="out_ln",
    )(a, w_out, x, mods, ln_g, ln_b)


def _seq_edge_rows(base, tm, seq, ctx, nlat):
    in_lat = base < nlat
    length = jnp.where(in_lat, seq, ctx)
    pos = jnp.where(in_lat, base % seq, (base - nlat) % ctx) + lax.broadcasted_iota(jnp.int32, (tm, 1), 0)
    first, last = pos == 0, pos == length - 1
    for k in range(1, tm // min(seq, ctx) + 1):
        first, last = first | (pos == k * length), last | (pos == k * length + length - 1)
    return first, last


def _ffn_up_kernel(a_ref, ap_ref, an_ref, wg_ref, wv_ref, cw_ref, cb_ref, o_ref, aext_ref, buf_ref,
                   *, tm, seq, ctx, nlat):
    halo = ap_ref.shape[0]

    @pl.when(pl.program_id(1) == 0)
    def _():
        aext_ref[:halo, :] = ap_ref[...]
        aext_ref[halo:halo + tm, :] = a_ref[...]
        aext_ref[halo + tm:, :] = an_ref[...]

    buf_ref[...] = _dot(aext_ref[...], wg_ref[...])
    val = _dot(a_ref[...], wv_ref[...])
    first, last = _seq_edge_rows(pl.program_id(0) * tm, tm, seq, ctx, nlat)
    cw = cw_ref[...]
    g_prev = jnp.where(first, 0.0, buf_ref[halo - 1:halo - 1 + tm, :])
    g_next = jnp.where(last, 0.0, buf_ref[halo + 1:halo + 1 + tm, :])
    gc = g_prev * cw[0:1] + buf_ref[halo:halo + tm, :] * cw[1:2] + g_next * cw[2:3] + cb_ref[...]
    o_ref[...] = (gc * jax.nn.sigmoid(gc) * val).astype(o_ref.dtype)


def _ffn_up(a, w_up, conv_w, conv_b, *, rows, seq, ctx, nlat):
    k = a.shape[1]
    dff = conv_w.shape[1]
    halo = 2 * SUBLANES
    tm = _tile((1024, 512, 256, 128), rows, seq)
    tf = _tile((512, 256, 128), dff)
    nf = dff // tf
    hb = tm // halo
    last_hb = rows // halo - 1
    return pl.pallas_call(
        functools.partial(_ffn_up_kernel, tm=tm, seq=seq, ctx=ctx, nlat=nlat),
        grid=(rows // tm, nf),
        in_specs=[pl.BlockSpec((tm, k), lambda i, j: (i, 0)),
                  pl.BlockSpec((halo, k), lambda i, j: (jnp.maximum(i * hb - 1, 0), 0)),
                  pl.BlockSpec((halo, k), lambda i, j: (jnp.minimum((i + 1) * hb, last_hb), 0)),
                  pl.BlockSpec((k, tf), lambda i, j: (0, j)),
                  pl.BlockSpec((k, tf), lambda i, j: (0, nf + j)),
                  pl.BlockSpec((3, tf), lambda i, j: (0, j)),
                  pl.BlockSpec((1, tf), lambda i, j: (0, j))],
        out_specs=pl.BlockSpec((tm, tf), lambda i, j: (i, j)),
        out_shape=jax.ShapeDtypeStruct((rows, dff), BF16),
        scratch_shapes=[pltpu.VMEM((tm + 2 * halo, k), BF16), pltpu.VMEM((tm + 2 * halo, tf), F32)],
        compiler_params=_cp("parallel", "arbitrary"), name="ffn_up",
    )(a, a, a, w_up, w_up, conv_w, conv_b)


def _down_ln_kernel(*refs, alpha, nk, with_next):
    if with_next:
        a_ref, w_ref, x_ref, m_ref, g_ref, b_ref, mn_ref, xo_ref, h_ref, acc_ref = refs
    else:
        a_ref, w_ref, x_ref, m_ref, g_ref, b_ref, xo_ref, acc_ref = refs
    k = pl.program_id(1)

    @pl.when(k == 0)
    def _():
        acc_ref[...] = jnp.zeros_like(acc_ref)

    acc_ref[...] += _dot(a_ref[...], w_ref[...])

    @pl.when(k == nk - 1)
    def _():
        xn = _post_ln_epilogue(acc_ref[...], x_ref[...], m_ref[...], 5, g_ref[...], b_ref[...], alpha)
        xo_ref[...] = xn
        if with_next:
            mn = mn_ref[...]
            h_ref[...] = (_ln(xn, ADA_EPS) * (1.0 + mn[1:2]) + mn[0:1]).astype(BF16)


def _down_ln(a, w_down, x, mods, ln_g, ln_b, mods_next, *, rows, seq, nb, alpha):
    d = x.shape[1]
    dff = a.shape[1]
    tm = _tile((512, 256, 128), rows, seq)
    nk = next(c for c in (4, 2, 1, 8, 11, 22, 44) if dff % c == 0 and (dff // c) % LANES == 0)
    tk = dff // nk
    with_next = mods_next is not None
    row = pl.BlockSpec((tm, d), lambda i, k: (i, 0))
    vec = pl.BlockSpec((1, d), lambda i, k: (0, 0))
    mod = pl.BlockSpec((None, 6, d), lambda i, k: (jnp.minimum(i * tm // seq, nb), 0, 0))
    specs = [pl.BlockSpec((tm, tk), lambda i, k: (i, k)), pl.BlockSpec((tk, d), lambda i, k: (k, 0)),
             row, mod, vec, vec]
    ins = [a, w_down, x, mods, ln_g, ln_b]
    out_specs = [row]
    out_shape = [jax.ShapeDtypeStruct((rows, d), F32)]
    if with_next:
        specs.append(mod)
        ins.append(mods_next)
        out_specs.append(row)
        out_shape.append(jax.ShapeDtypeStruct((rows, d), BF16))
    res = pl.pallas_call(
        functools.partial(_down_ln_kernel, alpha=alpha, nk=nk, with_next=with_next),
        grid=(rows // tm, nk), in_specs=specs, out_specs=out_specs, out_shape=out_shape,
        scratch_shapes=[pltpu.VMEM((tm, d), F32)],
        compiler_params=_cp("parallel", "arbitrary"), name="down_ln",
    )(*ins)
    return (res[0], res[1]) if with_next else (res[0], None)


def _rope_tables(seq, dims, half):
    t = jnp.arange(seq)
    rows, cols = (t // GRID_W).astype(F32), (t % GRID_W).astype(F32)
    j = jnp.arange(LANES)
    freqs = ROPE_THETA ** (-jnp.arange(half, dtype=F32) / half)
    f = freqs[j % half]
    use_row = (j % dims) < dims // 2
    ang = jnp.where(use_row[None, :], rows[:, None], cols[:, None]) * f[None, :]
    cos, sin = jnp.cos(ang), jnp.sin(ang)
    upper = (j % (2 * half)) < half
    return cos, jnp.where(upper[None, :], -sin, 0.0), jnp.where(upper[None, :], 0.0, sin)


def _rpb_rows(rpb):
    h, nr, nc = rpb.shape
    w = jnp.pad(rpb.astype(F32), ((0, 0), (0, 2 * SUBLANES - nr), (0, LANES - nc)))
    return jnp.roll(w, -(NA_KC - 1), axis=2)


def kernel(x, c, ctx, c_ctx, w_ada, b_ada, w_in, mla_kv_norm, w_mla_ukv, gqa_q_norm, gqa_k_norm, na_rpb,
           w_branch, w_out, ln_a_g, ln_a_b, w_up, conv_w, conv_b, w_down, ln_f_g, ln_f_b):
    nb, seq, d = x.shape
    lctx = ctx.shape[1]
    depth = w_in.shape[0]
    heads = d // HEAD_DIM
    gkv = heads // 4
    group = heads // gkv
    kvr = mla_kv_norm.shape[1]
    nlat, nctx = nb * seq, nb * lctx
    t_all = nlat + nctx
    alpha = float((2 * depth) ** 0.25)
    mla_scale = float((MLA_NOPE + MLA_ROPE) ** -0.5) * LOG2E
    hd_scale = float(HEAD_DIM ** -0.5) * LOG2E
    assert seq % (NA_QROWS * GRID_W) == 0 and seq // (NA_QROWS * GRID_W) >= 3
    assert heads % 4 == 0 and seq % lctx == 0 and nctx <= seq

    o_ckv = heads * (MLA_NOPE + MLA_ROPE)
    o_kr = o_ckv + kvr
    o_na = o_kr + MLA_ROPE
    o_gq = o_na + 3 * heads * HEAD_DIM
    o_gkv = o_gq + heads * HEAD_DIM
    o_gate = o_gkv + 2 * gkv * HEAD_DIM

    pad = (-(nb + 1)) % SUBLANES
    c_all = jnp.concatenate([c, c_ctx[None, :], jnp.zeros((pad, d), F32)], 0)
    mods = _ada(c_all, w_ada, b_ada).reshape(depth, nb + 1 + pad, 6, d)

    tm_proj = _tile((1024, 512, 256, 128), seq, nctx)
    rope_g = _rope_tables(seq, HEAD_DIM, HEAD_DIM // 4)
    rope_m = _rope_tables(seq, MLA_ROPE, MLA_ROPE // 4)

    xs, h = _ln_mod(x.reshape(nlat, d), ctx.reshape(nctx, d), mods[0], seq, nb)

    c_gq, c_naq, c_gate = 0, heads, 4 * heads
    c_mqn = c_gate + 3 * heads
    c_gv = c_mqn + heads
    c_gk = c_gv + gkv
    c_mqr = c_gk + gkv
    ones = lambda nblk: jnp.ones((nblk * HEAD_DIM,), F32)
    cs_proj = jnp.concatenate([ones(heads), ones(heads) * hd_scale, ones(5 * heads), ones(heads) * mla_scale,
                               ones(2 * gkv), ones(heads // 2) * mla_scale])[None, :]

    for l in range(depth):
        last = l == depth - 1
        rows = nlat if last else t_all
        wl = w_in[l]
        mq = wl[:, :o_ckv].reshape(d, heads, MLA_NOPE + MLA_ROPE)
        gkvw = wl[:, o_gkv:o_gate].reshape(d, 2, gkv * HEAD_DIM)
        w_proj = jnp.concatenate(
            [wl[:, o_gq:o_gkv], wl[:, o_na:o_gq], wl[:, o_gate:], mq[:, :, :MLA_NOPE].reshape(d, heads * MLA_NOPE),
             gkvw[:, 1], gkvw[:, 0], mq[:, :, MLA_NOPE:].reshape(d, heads * MLA_ROPE)], 1).astype(BF16)
        w_kr2 = jnp.concatenate([wl[:, o_kr:o_na], wl[:, o_kr:o_na]], 1).astype(BF16)

        p = _mm(h, w_proj, tm=tm_proj, cs=cs_proj, name="proj")
        p_ckv, p_kr = _proj_ckv(h, wl[:, o_ckv:o_kr].astype(BF16), mla_kv_norm[l][None, :], w_kr2, tm=tm_proj)
        p_kv = _mm(p_ckv, w_mla_ukv[l].astype(BF16), tm=tm_proj, name="mla_ukv",
                   tn=_tile((2048, 1024, 512, 256, 128), w_mla_ukv.shape[2]))

        dims = dict(nb=nb, seq=seq, ctx=lctx)
        gains = ((gqa_q_norm[l] * hd_scale)[None, :], gqa_k_norm[l][None, :])
        ya = _mla(p, c_mqn // 2, c_mqr, p_kv, p_kr, heads=heads, with_lat=True, out_rows=rows, name="mla_lat",
                  tabs=rope_m, **dims)
        yb = _na(p, c_naq, _rpb_rows(na_rpb[l]), heads=heads, out_rows=rows, **dims)
        yc = _gattn(p, c_gq // group, p, c_gk, p, c_gv, group=group, n_kv=gkv, with_lat=True, out_rows=rows,
                    name="gqa_lat", gains=gains, tabs=rope_g, **dims)
        if not last:
            ya_c = _mla(p, c_mqn // 2, c_mqr, p_kv, p_kr, heads=heads, with_lat=False, out_rows=nctx,
                        name="mla_ctx", **dims)
            yb_c = _gattn(p, c_naq, p, c_naq + heads, p, c_naq + 2 * heads, group=1, n_kv=heads, with_lat=False,
                          out_rows=nctx, name="na_ctx", **dims)
            yc_c = _gattn(p, c_gq // group, p, c_gk, p, c_gv, group=group, n_kv=gkv, with_lat=False,
                          out_rows=nctx, name="gqa_ctx", gains=gains, **dims)
            ya = lax.dynamic_update_slice(ya, ya_c, (nlat, 0))
            yb = lax.dynamic_update_slice(yb, yb_c, (nlat, 0))
            yc = lax.dynamic_update_slice(yc, yc_c, (nlat, 0))

        merged = _merge(ya, yb, yc, w_branch[l].astype(BF16), p, c_gate * HEAD_DIM, rows=rows)
        xs, h2 = _out_ln(merged, w_out[l].astype(BF16), xs, mods[l], ln_a_g[l][None, :], ln_a_b[l][None, :],
                         rows=rows, seq=seq, nb=nb, alpha=alpha)
        hid = _ffn_up(h2, w_up[l].astype(BF16), conv_w[l], conv_b[l][None, :], rows=rows, seq=seq, ctx=lctx,
                      nlat=nlat)
        xs, h = _down_ln(hid, w_down[l].astype(BF16), xs, mods[l], ln_f_g[l][None, :], ln_f_b[l][None, :],
                         None if last else mods[l + 1], rows=rows, seq=seq, nb=nb, alpha=alpha)
    return xs[:nlat].reshape(nb, seq, d)
```

```python
import functools
import math

import jax
import jax.numpy as jnp
from jax import lax
from jax.experimental import pallas as pl
from jax.experimental.pallas import tpu as pltpu

F32 = jnp.float32
BF16 = jnp.bfloat16

GRID_W = 64
HEAD_DIM = 128
MLA_NOPE = 128
MLA_ROPE = 64
NA_KR = 8
NA_KC = 16
NA_QROWS = 4
NA_WROWS = NA_QROWS + NA_KR
ROPE_THETA = 10000.0
ADA_EPS = 1e-6
POST_EPS = 1e-5
RMS_EPS = 1e-6
NEG_INF = -1e30
LOG2E = math.log2(math.e)
LANES = 128
SUBLANES = 8
VMEM_LIMIT = 56 * 2**20
ATTN_TQ = 256


def _cp(*sem):
    return pltpu.CompilerParams(dimension_semantics=sem, vmem_limit_bytes=VMEM_LIMIT)


def _tile(cands, *ns):
    for c in cands:
        if all(n % c == 0 for n in ns):
            return c
    raise ValueError(f"no tile in {cands} divides {ns}")


def _ln(x, eps):
    mu = jnp.mean(x, -1, keepdims=True)
    xc = x - mu
    var = jnp.mean(xc * xc, -1, keepdims=True)
    return xc * lax.rsqrt(var + eps)


def _rms(x):
    return x * lax.rsqrt(jnp.mean(x * x, -1, keepdims=True) + RMS_EPS)


def _rope(x, tabs, shift):
    cos, sa, sb = tabs
    return x * cos + pltpu.roll(x, LANES - shift, 1) * sa + pltpu.roll(x, shift, 1) * sb


def _dot(a, b):
    return jnp.dot(a, b, preferred_element_type=F32)


def _dot_nt(a, b):
    return lax.dot_general(a, b, (((1,), (1,)), ((), ())), preferred_element_type=F32)


def _ada_kernel(c_ref, w_ref, b_ref, o_ref):
    c = c_ref[...]
    a = (c * jax.nn.sigmoid(c)).astype(BF16)
    o_ref[...] = _dot(a, w_ref[...].astype(BF16)) + b_ref[...]


def _ada(c_all, w_ada, b_ada):
    depth, d, n = w_ada.shape
    r = c_all.shape[0]
    tn = _tile((1024, 512, 256, 128), n)
    return pl.pallas_call(
        _ada_kernel,
        grid=(depth, n // tn),
        in_specs=[pl.BlockSpec((r, d), lambda l, j: (0, 0)),
                  pl.BlockSpec((None, d, tn), lambda l, j: (l, 0, j)),
                  pl.BlockSpec((None, 1, tn), lambda l, j: (l, 0, j))],
        out_specs=pl.BlockSpec((None, r, tn), lambda l, j: (l, 0, j)),
        out_shape=jax.ShapeDtypeStruct((depth, r, n), F32),
        compiler_params=_cp("parallel", "parallel"),
        name="ada",
    )(c_all, w_ada, b_ada.reshape(depth, 1, n))


def _ln_mod_kernel(x_ref, c_ref, m_ref, xs_ref, h_ref, *, nlat_tiles):
    m = m_ref[...]

    def emit(src_ref):
        x = src_ref[...]
        xs_ref[...] = x
        h_ref[...] = (_ln(x, ADA_EPS) * (1.0 + m[1:2]) + m[0:1]).astype(BF16)

    pl.when(pl.program_id(0) < nlat_tiles)(lambda: emit(x_ref))
    pl.when(pl.program_id(0) >= nlat_tiles)(lambda: emit(c_ref))


def _ln_mod(x, ctx, mods, seq, nb):
    nlat, d = x.shape
    nctx = ctx.shape[0]
    tm = _tile((512, 256, 128), seq, nctx)
    nl = nlat // tm
    row = pl.BlockSpec((tm, d), lambda i: (i, 0))
    return pl.pallas_call(
        functools.partial(_ln_mod_kernel, nlat_tiles=nl),
        grid=((nlat + nctx) // tm,),
        in_specs=[pl.BlockSpec((tm, d), lambda i: (jnp.minimum(i, nl - 1), 0)),
                  pl.BlockSpec((tm, d), lambda i: (jnp.maximum(i - nl, 0), 0)),
                  pl.BlockSpec((None, 6, d), lambda i: (jnp.minimum(i * tm // seq, nb), 0, 0))],
        out_specs=[row, row],
        out_shape=[jax.ShapeDtypeStruct((nlat + nctx, d), F32), jax.ShapeDtypeStruct((nlat + nctx, d), BF16)],
        compiler_params=_cp("arbitrary"),
        name="ln_mod",
    )(x, ctx, mods)


def _mm_kernel(a_ref, w_ref, o_ref):
    o_ref[...] = _dot(a_ref[...], w_ref[...]).astype(o_ref.dtype)


def _mm_scale_kernel(a_ref, w_ref, cs_ref, o_ref):
    o_ref[...] = (_dot(a_ref[...], w_ref[...]) * cs_ref[...]).astype(o_ref.dtype)


def _mm(a, w, *, rows=None, tm=None, tn=None, cs=None, name="mm"):
    t, k = a.shape
    n = w.shape[1]
    rows = t if rows is None else rows
    tm = tm or _tile((1024, 512, 256, 128), rows)
    tn = tn or _tile((1024, 512, 256, 128), n)
    a_spec = pl.BlockSpec((tm, k), lambda i, j: (i, 0))
    w_spec = pl.BlockSpec((k, tn), lambda i, j: (0, j))
    if cs is not None:
        kern, ins, specs = _mm_scale_kernel, (a, w, cs), [a_spec, w_spec, pl.BlockSpec((1, tn), lambda i, j: (0, j))]
    else:
        kern, ins, specs = _mm_kernel, (a, w), [a_spec, w_spec]
    return pl.pallas_call(
        kern, grid=(rows // tm, n // tn), in_specs=specs,
        out_specs=pl.BlockSpec((tm, tn), lambda i, j: (i, j)),
        out_shape=jax.ShapeDtypeStruct((rows, n), BF16),
        compiler_params=_cp("parallel", "parallel"), name=name,
    )(*ins)


def _ckv_kernel(a_ref, wc_ref, g_ref, wr_ref, oc_ref, or_ref):
    a = a_ref[...]
    oc_ref[...] = (_rms(_dot(a, wc_ref[...])) * g_ref[...]).astype(oc_ref.dtype)
    or_ref[...] = _dot(a, wr_ref[...]).astype(or_ref.dtype)


def _proj_ckv(a, w_ckv, gain, w_kr2, *, tm):
    t, k = a.shape
    r = w_ckv.shape[1]
    full = lambda shape: pl.BlockSpec(shape, lambda i: (0, 0))
    return pl.pallas_call(
        _ckv_kernel, grid=(t // tm,),
        in_specs=[pl.BlockSpec((tm, k), lambda i: (i, 0)), full((k, r)), full((1, r)), full((k, LANES))],
        out_specs=[pl.BlockSpec((tm, r), lambda i: (i, 0)), pl.BlockSpec((tm, LANES), lambda i: (i, 0))],
        out_shape=[jax.ShapeDtypeStruct((t, r), BF16), jax.ShapeDtypeStruct((t, LANES), BF16)],
        compiler_params=_cp("parallel"), name="proj_ckv",
    )(a, w_ckv, gain, w_kr2)


def _softmax_pv(s, v):
    e = jnp.exp2(s - jnp.max(s, -1, keepdims=True))
    den = jnp.sum(e, -1, keepdims=True)
    return _dot(e.astype(BF16), v) * (1.0 / den)


def _softmax_pv_ones(s, v_ext):
    e = jnp.exp2(s - jnp.max(s, -1, keepdims=True)).astype(BF16)
    o = _dot(e, v_ext)
    return o[:, :HEAD_DIM] * (1.0 / o[:, HEAD_DIM:HEAD_DIM + 1])


def _stage_values(vf_ref, v_lat, v_ctx):
    nl = v_lat.shape[0]
    vf_ref[:nl, :HEAD_DIM] = v_lat
    vf_ref[nl:, :HEAD_DIM] = v_ctx
    lane = lax.broadcasted_iota(jnp.int32, (vf_ref.shape[0], LANES), 1)
    vf_ref[:, HEAD_DIM:] = jnp.where(lane == 0, 1.0, 0.0).astype(vf_ref.dtype)


def _chunk_rows(i, tq):
    return pl.ds(i * tq, tq) if isinstance(i, int) else pl.ds(pl.multiple_of(i * tq, tq), tq)


def _pipelined_chunks(nq, qk, sm_pv):
    qk(0, 0)
    npairs = (nq - 1) // 2

    def body(j, carry):
        i = 2 * j
        qk(i + 1, 1)
        sm_pv(i, 0)
        qk(i + 2, 0)
        sm_pv(i + 1, 1)
        return carry

    lax.fori_loop(0, npairs, body, 0)
    i = 2 * npairs
    if i + 1 < nq:
        qk(i + 1, 1)
        sm_pv(i, 0)
        sm_pv(i + 1, 1)
    else:
        sm_pv(i, 0)


def _gattn_lat_kernel(q_ref, kl_ref, vl_ref, kc_ref, vc_ref, qg_ref, kg_ref, cos_ref, sa_ref, sb_ref,
                      o_ref, kf_ref, vf_ref, s0_ref, s1_ref, *, group, tq):
    nl = kl_ref.shape[0]

    def prep(x, g_ref, tabs):
        x = _rms(x.astype(F32)) * g_ref[...]
        if tabs is not None:
            x = _rope(x, tabs, HEAD_DIM // 4)
        return x.astype(BF16)

    kf_ref[:nl, :] = prep(kl_ref[...], kg_ref, (cos_ref[...], sa_ref[...], sb_ref[...]))
    kf_ref[nl:, :] = prep(kc_ref[...], kg_ref, None)
    _stage_values(vf_ref, vl_ref[...], vc_ref[...])

    s_refs = (s0_ref, s1_ref)

    def qk(i, slot):
        rows = _chunk_rows(i, tq)
        tabs = (cos_ref[rows, :], sa_ref[rows, :], sb_ref[rows, :])
        for g in range(group):
            q = prep(q_ref[rows, g * HEAD_DIM:(g + 1) * HEAD_DIM], qg_ref, tabs)
            s_refs[slot][g] = _dot_nt(q, kf_ref[...])

    def sm_pv(i, slot):
        rows = _chunk_rows(i, tq)
        for g in range(group):
            o = _softmax_pv_ones(s_refs[slot][g], vf_ref[...])
            o_ref[rows, g * HEAD_DIM:(g + 1) * HEAD_DIM] = o.astype(o_ref.dtype)

    _pipelined_chunks(nl // tq, qk, sm_pv)


def _gattn_ctx_kernel(*refs, group, rms):
    if rms:
        q_ref, kc_ref, vc_ref, qg_ref, kg_ref, o_ref = refs
        norm = lambda x, g_ref: (_rms(x.astype(F32)) * g_ref[...]).astype(BF16)
        k = norm(kc_ref[...], kg_ref)
    else:
        q_ref, kc_ref, vc_ref, o_ref = refs
        k = kc_ref[...]
    for g in range(group):
        sl = slice(g * HEAD_DIM, (g + 1) * HEAD_DIM)
        q = norm(q_ref[:, sl], qg_ref) if rms else q_ref[:, sl]
        o_ref[:, sl] = _softmax_pv(_dot_nt(q, k), vc_ref[...]).astype(o_ref.dtype)


def _gattn(q_arr, q_col, k_arr, k_col, v_arr, v_col, *, group, n_kv, nb, seq, ctx, with_lat, out_rows, name,
           gains=None, tabs=None):
    nlat_c = nb * seq // ctx
    gw = group * HEAD_DIM
    kc_spec = pl.BlockSpec((ctx, HEAD_DIM), lambda b, h: (nlat_c + b, k_col + h))
    vc_spec = pl.BlockSpec((ctx, HEAD_DIM), lambda b, h: (nlat_c + b, v_col + h))
    g_spec = pl.BlockSpec((1, HEAD_DIM), lambda b, h: (0, 0))
    if with_lat:
        tq = _tile((ATTN_TQ, 128), seq)
        t_spec = pl.BlockSpec((seq, LANES), lambda b, h: (0, 0))
        specs = [pl.BlockSpec((seq, gw), lambda b, h: (b, q_col + h)),
                 pl.BlockSpec((seq, HEAD_DIM), lambda b, h: (b, k_col + h)),
                 pl.BlockSpec((seq, HEAD_DIM), lambda b, h: (b, v_col + h)),
                 kc_spec, vc_spec, g_spec, g_spec, t_spec, t_spec, t_spec]
        ins = [q_arr, k_arr, v_arr, k_arr, v_arr, *gains, *tabs]
        kern = functools.partial(_gattn_lat_kernel, group=group, tq=tq)
        scratch = [pltpu.VMEM((seq + ctx, HEAD_DIM), BF16), pltpu.VMEM((seq + ctx, 2 * HEAD_DIM), BF16),
                   pltpu.VMEM((group, tq, seq + ctx), F32), pltpu.VMEM((group, tq, seq + ctx), F32)]
        o_spec = pl.BlockSpec((seq, gw), lambda b, h: (b, h))
    else:
        rms = gains is not None
        specs = [pl.BlockSpec((ctx, gw), lambda b, h: (nlat_c + b, q_col + h)), kc_spec, vc_spec]
        ins = [q_arr, k_arr, v_arr]
        if rms:
            specs += [g_spec, g_spec]
            ins += list(gains)
        kern = functools.partial(_gattn_ctx_kernel, group=group, rms=rms)
        scratch = []
        o_spec = pl.BlockSpec((ctx, gw), lambda b, h: (b, h))
    return pl.pallas_call(
        kern, grid=(nb, n_kv), in_specs=specs, out_specs=o_spec,
        out_shape=jax.ShapeDtypeStruct((out_rows, n_kv * gw), BF16),
        scratch_shapes=scratch,
        compiler_params=_cp("parallel", "parallel"), name=name,
    )(*ins)


def _mla_q(qn, qr, hh, first_half):
    own = first_half if hh == 0 else jnp.logical_not(first_half)
    return jnp.concatenate([qn, jnp.where(own, qr, 0.0).astype(BF16)], axis=1)


def _mla_lat_kernel(qn_ref, qr_ref, kvl_ref, krl_ref, kvc_ref, krc_ref, cos_ref, sa_ref, sb_ref,
                    o_ref, kf_ref, vf_ref, s0_ref, s1_ref, *, tq):
    nl = kvl_ref.shape[0]
    kr = _rope(krl_ref[...].astype(F32), (cos_ref[...], sa_ref[...], sb_ref[...]), MLA_ROPE // 4).astype(BF16)
    for hh in range(2):
        c0 = hh * 2 * HEAD_DIM
        kf_ref[hh, :nl, :MLA_NOPE] = kvl_ref[:, c0:c0 + MLA_NOPE]
        kf_ref[hh, :nl, MLA_NOPE:] = kr
        kf_ref[hh, nl:, :MLA_NOPE] = kvc_ref[:, c0:c0 + MLA_NOPE]
        kf_ref[hh, nl:, MLA_NOPE:] = krc_ref[...]
        _stage_values(vf_ref.at[hh], kvl_ref[:, c0 + MLA_NOPE:c0 + 2 * HEAD_DIM],
                      kvc_ref[:, c0 + MLA_NOPE:c0 + 2 * HEAD_DIM])
    first_half = lax.broadcasted_iota(jnp.int32, (tq, LANES), 1) < MLA_ROPE

    s_refs = (s0_ref, s1_ref)

    def qk(i, slot):
        rows = _chunk_rows(i, tq)
        qr = _rope(qr_ref[rows, :].astype(F32), (cos_ref[rows, :], sa_ref[rows, :], sb_ref[rows, :]), MLA_ROPE // 4)
        for hh in range(2):
            q = _mla_q(qn_ref[rows, hh * MLA_NOPE:(hh + 1) * MLA_NOPE], qr, hh, first_half)
            s_refs[slot][hh] = _dot_nt(q, kf_ref[hh])

    def sm_pv(i, slot):
        rows = _chunk_rows(i, tq)
        for hh in range(2):
            o = _softmax_pv_ones(s_refs[slot][hh], vf_ref[hh])
            o_ref[rows, hh * HEAD_DIM:(hh + 1) * HEAD_DIM] = o.astype(o_ref.dtype)

    _pipelined_chunks(nl // tq, qk, sm_pv)


def _mla_ctx_kernel(qn_ref, qr_ref, kvc_ref, krc_ref, o_ref):
    qr = qr_ref[...].astype(F32)
    first_half = lax.broadcasted_iota(jnp.int32, qr.shape, 1) < MLA_ROPE
    for hh in range(2):
        c0 = hh * 2 * HEAD_DIM
        q = _mla_q(qn_ref[:, hh * MLA_NOPE:(hh + 1) * MLA_NOPE], qr, hh, first_half)
        k = jnp.concatenate([kvc_ref[:, c0:c0 + MLA_NOPE], krc_ref[...]], axis=1)
        v = kvc_ref[:, c0 + MLA_NOPE:c0 + 2 * HEAD_DIM]
        o_ref[:, hh * HEAD_DIM:(hh + 1) * HEAD_DIM] = _softmax_pv(_dot_nt(q, k), v).astype(o_ref.dtype)


def _mla(p_arr, qn_col, qr_col, kv_arr, kr_arr, *, heads, nb, seq, ctx, with_lat, out_rows, name, tabs=None):
    nlat_c = nb * seq // ctx
    qrow = (lambda b: b) if with_lat else (lambda b: nlat_c + b)
    qlen = seq if with_lat else ctx
    specs = [pl.BlockSpec((qlen, 2 * MLA_NOPE), lambda b, h: (qrow(b), qn_col + h)),
             pl.BlockSpec((qlen, 2 * MLA_ROPE), lambda b, h: (qrow(b), qr_col + h))]
    ins = [p_arr, p_arr]
    if with_lat:
        specs += [pl.BlockSpec((seq, 4 * HEAD_DIM), lambda b, h: (b, h)),
                  pl.BlockSpec((seq, LANES), lambda b, h: (b, 0))]
        ins += [kv_arr, kr_arr]
    specs += [pl.BlockSpec((ctx, 4 * HEAD_DIM), lambda b, h: (nlat_c + b, h)),
              pl.BlockSpec((ctx, LANES), lambda b, h: (nlat_c + b, 0))]
    ins += [kv_arr, kr_arr]
    if with_lat:
        tq = _tile((ATTN_TQ, 128), seq)
        specs += [pl.BlockSpec((seq, LANES), lambda b, h: (0, 0))] * 3
        ins += list(tabs)
        kern = functools.partial(_mla_lat_kernel, tq=tq)
        scratch = [pltpu.VMEM((2, seq + ctx, 2 * HEAD_DIM), BF16), pltpu.VMEM((2, seq + ctx, 2 * HEAD_DIM), BF16),
                   pltpu.VMEM((2, tq, seq + ctx), F32), pltpu.VMEM((2, tq, seq + ctx), F32)]
    else:
        kern, scratch = _mla_ctx_kernel, []
    return pl.pallas_call(
        kern, grid=(nb, heads // 2), in_specs=specs,
        out_specs=pl.BlockSpec((qlen, 2 * HEAD_DIM), lambda b, h: (b, h)),
        out_shape=jax.ShapeDtypeStruct((out_rows, heads * HEAD_DIM), BF16),
        scratch_shapes=scratch,
        compiler_params=_cp("parallel", "parallel"), name=name,
    )(*ins)


def _na_fill_bias(w_ref, bias_ref):
    cq = lax.broadcasted_iota(jnp.int32, (GRID_W, LANES), 0)
    lane = lax.broadcasted_iota(jnp.int32, (GRID_W, LANES), 1)
    ck = lane % GRID_W
    cs = jnp.clip(cq - NA_KC // 2, 0, GRID_W - NA_KC)
    col_ok = (ck >= cs) & (ck < cs + NA_KC)
    even = lane < GRID_W
    ninf = jnp.full((GRID_W, LANES), NEG_INF, F32)
    w = w_ref[...] * LOG2E
    toe = {}

    def toeplitz(a, odd):
        if (a, odd) not in toe:
            wa = jnp.broadcast_to(w[a:a + 1, :], (GRID_W, LANES))
            toe[(a, odd)] = pltpu.roll(wa, GRID_W if odd else 0, 1, stride=1, stride_axis=0)
        return toe[(a, odd)]

    for kind, r0_rel in enumerate((0, NA_KR // 2, NA_WROWS - NA_QROWS)):
        for rq in range(NA_QROWS):
            rs_rel = (0, rq, NA_WROWS - NA_KR)[kind]
            for pair in range(NA_WROWS // 2):
                halves = []
                for odd in (0, 1):
                    rk = 2 * pair + odd
                    if rs_rel <= rk < rs_rel + NA_KR:
                        halves.append(toeplitz(rk - r0_rel - rq + NA_KR - 1, odd))
                    else:
                        halves.append(ninf)
                blk = jnp.where(col_ok, jnp.where(even, halves[0], halves[1]), NEG_INF)
                bias_ref[kind, rq * GRID_W:(rq + 1) * GRID_W, pair * LANES:(pair + 1) * LANES] = blk


def _na_kernel(q_ref, k_ref, v_ref, kc_ref, vc_ref, w_ref, o_ref, bias_ref, *, nblk):
    qt = NA_QROWS * GRID_W
    wt = NA_WROWS * GRID_W

    @pl.when(pl.program_id(1) == 0)
    def _():
        _na_fill_bias(w_ref, bias_ref)

    kc, vc = kc_ref[...], vc_ref[...]
    win = lambda rb: slice(min(max(rb - 1, 0), nblk - 3) * qt, min(max(rb - 1, 0), nblk - 3) * qt + wt)

    def scores(rb):
        q = q_ref[rb * qt:(rb + 1) * qt, :]
        kind = 0 if rb == 0 else (2 if rb == nblk - 1 else 1)
        return _dot_nt(q, k_ref[win(rb), :]) + bias_ref[kind], _dot_nt(q, kc)

    nxt = scores(0)
    for rb in range(nblk):
        s_lat, s_ctx = nxt
        if rb + 1 < nblk:
            nxt = scores(rb + 1)
        m = jnp.maximum(jnp.max(s_lat, -1, keepdims=True), jnp.max(s_ctx, -1, keepdims=True))
        e_lat, e_ctx = jnp.exp2(s_lat - m), jnp.exp2(s_ctx - m)
        den = jnp.sum(e_lat, -1, keepdims=True) + jnp.sum(e_ctx, -1, keepdims=True)
        o = _dot(e_lat.astype(BF16), v_ref[win(rb), :]) + _dot(e_ctx.astype(BF16), vc)
        o_ref[rb * qt:(rb + 1) * qt, :] = (o * (1.0 / den)).astype(o_ref.dtype)


def _na(p_arr, q_col, rpb_w, *, heads, nb, seq, ctx, out_rows):
    nlat_c = nb * seq // ctx
    nblk = seq // (NA_QROWS * GRID_W)
    qt, wt = NA_QROWS * GRID_W, NA_WROWS * GRID_W
    lat = lambda off: pl.BlockSpec((seq, HEAD_DIM), lambda h, b: (b, q_col + off + h))
    cx = lambda off: pl.BlockSpec((ctx, HEAD_DIM), lambda h, b: (nlat_c + b, q_col + off + h))
    return pl.pallas_call(
        functools.partial(_na_kernel, nblk=nblk),
        grid=(heads, nb),
        in_specs=[lat(0), lat(heads), lat(2 * heads), cx(heads), cx(2 * heads),
                  pl.BlockSpec((None,) + rpb_w.shape[1:], lambda h, b: (h, 0, 0))],
        out_specs=pl.BlockSpec((seq, HEAD_DIM), lambda h, b: (b, h)),
        out_shape=jax.ShapeDtypeStruct((out_rows, heads * HEAD_DIM), BF16),
        scratch_shapes=[pltpu.VMEM((3, qt, wt), F32)],
        compiler_params=_cp("parallel", "arbitrary"), name="na_lat",
    )(p_arr, p_arr, p_arr, p_arr, p_arr, rpb_w)


def _merge_kernel(*refs, nlat_tiles):
    if nlat_tiles is None:
        ya_ref, yb_ref, yc_ref, w_ref, ga_ref, gb_ref, gc_ref, o_ref = refs
        ctx_refs = None
    else:
        ya_ref, yb_ref, yc_ref, ca_ref, cb_ref, cc_ref, w_ref, ga_ref, gb_ref, gc_ref, o_ref = refs
        ctx_refs = (ca_ref, cb_ref, cc_ref)

    def emit(y_refs):
        acc = None
        for i, (y_ref, g_ref) in enumerate(zip(y_refs, (ga_ref, gb_ref, gc_ref))):
            t = jax.nn.sigmoid(g_ref[...].astype(F32)) * _dot(y_ref[...], w_ref[i])
            acc = t if acc is None else acc + t
        o_ref[...] = acc.astype(o_ref.dtype)

    if ctx_refs is None:
        emit((ya_ref, yb_ref, yc_ref))
    else:
        pl.when(pl.program_id(0) < nlat_tiles)(lambda: emit((ya_ref, yb_ref, yc_ref)))
        pl.when(pl.program_id(0) >= nlat_tiles)(lambda: emit(ctx_refs))


def _merge(ys_lat, ys_ctx, w_branch, p_arr, gate_col, *, rows):
    nlat, d = ys_lat[0].shape
    n = w_branch.shape[2]
    tm = _tile((512, 256, 128), nlat, rows)
    tn = _tile((512, 256, 128), n)
    nl = nlat // tm
    g_spec = lambda b: pl.BlockSpec((tm, tn), lambda i, j: (i, (gate_col + b * n) // tn + j))
    specs = [pl.BlockSpec((tm, d), lambda i, j: (jnp.minimum(i, nl - 1), 0))] * 3
    ins = list(ys_lat)
    if ys_ctx is not None:
        specs += [pl.BlockSpec((tm, d), lambda i, j: (jnp.maximum(i - nl, 0), 0))] * 3
        ins += list(ys_ctx)
    specs += [pl.BlockSpec((3, d, tn), lambda i, j: (0, 0, j)), g_spec(0), g_spec(1), g_spec(2)]
    ins += [w_branch, p_arr, p_arr, p_arr]
    return pl.pallas_call(
        functools.partial(_merge_kernel, nlat_tiles=None if ys_ctx is None else nl),
        grid=(rows // tm, n // tn), in_specs=specs,
        out_specs=pl.BlockSpec((tm, tn), lambda i, j: (i, j)),
        out_shape=jax.ShapeDtypeStruct((rows, n), BF16),
        compiler_params=_cp("parallel", "parallel"), name="merge",
    )(*ins)


def _post_ln_epilogue(y, x, m, gate_row, g, b, alpha):
    z = alpha * x + m[gate_row:gate_row + 1] * y
    return _ln(z, POST_EPS) * g + b


def _out_ln_kernel(a_ref, w_ref, x_ref, m_ref, g_ref, b_ref, xo_ref, h_ref, y0_ref, y1_ref, *, alpha):
    i = pl.program_id(0)

    @pl.when(i == 0)
    def _():
        y1_ref[...] = jnp.zeros_like(y1_ref)

    def step(y_cur, y_prev):
        y_cur[...] = _dot(a_ref[...], w_ref[...])
        m = m_ref[...]
        xn = _post_ln_epilogue(y_prev[...], x_ref[...], m, 2, g_ref[...], b_ref[...], alpha)
        xo_ref[...] = xn
        h_ref[...] = (_ln(xn, ADA_EPS) * (1.0 + m[4:5]) + m[3:4]).astype(BF16)

    pl.when(i % 2 == 0)(lambda: step(y0_ref, y1_ref))
    pl.when(i % 2 == 1)(lambda: step(y1_ref, y0_ref))


def _out_ln(a, w_out, x, mods, ln_g, ln_b, *, rows, seq, nb, alpha):
    d = x.shape[1]
    tm = _tile((256, 128), rows, seq)
    nt = rows // tm
    lag = lambda i: jnp.maximum(i - 1, 0)
    row = pl.BlockSpec((tm, d), lambda i: (lag(i), 0))
    vec = pl.BlockSpec((1, d), lambda i: (0, 0))
    return pl.pallas_call(
        functools.partial(_out_ln_kernel, alpha=alpha), grid=(nt + 1,),
        in_specs=[pl.BlockSpec((tm, d), lambda i: (jnp.minimum(i, nt - 1), 0)),
                  pl.BlockSpec((d, d), lambda i: (0, 0)), row,
                  pl.BlockSpec((None, 6, d), lambda i: (jnp.minimum(lag(i) * tm // seq, nb), 0, 0)), vec, vec],
        out_specs=[row, row],
        out_shape=[jax.ShapeDtypeStruct((rows, d), F32), jax.ShapeDtypeStruct((rows, d), BF16)],
        scratch_shapes=[pltpu.VMEM((tm, d), F32), pltpu.VMEM((tm, d), F32)],
        compiler_params=_cp("arbitrary"), name="out_ln",
    )(a, w_out, x, mods, ln_g, ln_b)


def _seq_edge_rows(base, tm, seq, ctx, nlat):
    in_lat = base < nlat
    length = jnp.where(in_lat, seq, ctx)
    pos = jnp.where(in_lat, base % seq, (base - nlat) % ctx) + lax.broadcasted_iota(jnp.int32, (tm, 1), 0)
    first, last = pos == 0, pos == length - 1
    for k in range(1, tm // min(seq, ctx) + 1):
        first, last = first | (pos == k * length), last | (pos == k * length + length - 1)
    return first, last


def _ffn_up_kernel(a_ref, ap_ref, an_ref, wg_ref, wv_ref, cw_ref, cb_ref, o_ref, aext_ref, buf_ref,
                   *, tm, seq, ctx, nlat):
    halo = ap_ref.shape[0]

    @pl.when(pl.program_id(1) == 0)
    def _():
        aext_ref[:halo, :] = ap_ref[...]
        aext_ref[halo:halo + tm, :] = a_ref[...]
        aext_ref[halo + tm:, :] = an_ref[...]

    buf_ref[...] = _dot(aext_ref[...], wg_ref[...])
    val = _dot(a_ref[...], wv_ref[...])
    first, last = _seq_edge_rows(pl.program_id(0) * tm, tm, seq, ctx, nlat)
    cw = cw_ref[...]
    g_prev = jnp.where(first, 0.0, buf_ref[halo - 1:halo - 1 + tm, :])
    g_next = jnp.where(last, 0.0, buf_ref[halo + 1:halo + 1 + tm, :])
    gc = g_prev * cw[0:1] + buf_ref[halo:halo + tm, :] * cw[1:2] + g_next * cw[2:3] + cb_ref[...]
    o_ref[...] = (gc * jax.nn.sigmoid(gc) * val).astype(o_ref.dtype)


def _ffn_up(a, w_up, conv_w, conv_b, *, rows, seq, ctx, nlat):
    k = a.shape[1]
    dff = conv_w.shape[1]
    halo = 2 * SUBLANES
    tm = _tile((1024, 512, 256, 128), rows, seq)
    tf = _tile((512, 256, 128), dff)
    nf = dff // tf
    hb = tm // halo
    last_hb = rows // halo - 1
    return pl.pallas_call(
        functools.partial(_ffn_up_kernel, tm=tm, seq=seq, ctx=ctx, nlat=nlat),
        grid=(rows // tm, nf),
        in_specs=[pl.BlockSpec((tm, k), lambda i, j: (i, 0)),
                  pl.BlockSpec((halo, k), lambda i, j: (jnp.maximum(i * hb - 1, 0), 0)),
                  pl.BlockSpec((halo, k), lambda i, j: (jnp.minimum((i + 1) * hb, last_hb), 0)),
                  pl.BlockSpec((k, tf), lambda i, j: (0, j)),
                  pl.BlockSpec((k, tf), lambda i, j: (0, nf + j)),
                  pl.BlockSpec((3, tf), lambda i, j: (0, j)),
                  pl.BlockSpec((1, tf), lambda i, j: (0, j))],
        out_specs=pl.BlockSpec((tm, tf), lambda i, j: (i, j)),
        out_shape=jax.ShapeDtypeStruct((rows, dff), BF16),
        scratch_shapes=[pltpu.VMEM((tm + 2 * halo, k), BF16), pltpu.VMEM((tm + 2 * halo, tf), F32)],
        compiler_params=_cp("parallel", "arbitrary"), name="ffn_up",
    )(a, a, a, w_up, w_up, conv_w, conv_b)


def _down_ln_kernel(*refs, alpha, nk, with_next):
    if with_next:
        a_ref, w_ref, x_ref, m_ref, g_ref, b_ref, mn_ref, xo_ref, h_ref, acc0_ref, acc1_ref = refs
    else:
        a_ref, w_ref, x_ref, m_ref, g_ref, b_ref, xo_ref, acc0_ref, acc1_ref = refs
    i, k = pl.program_id(0), pl.program_id(1)
    part = acc0_ref.shape[0] // nk
    rows = pl.ds(pl.multiple_of(k * part, part), part)

    @pl.when((i == 0) & (k == 0))
    def _():
        acc1_ref[...] = jnp.zeros_like(acc1_ref)

    def step(acc_cur, acc_prev):
        @pl.when(k == 0)
        def _():
            acc_cur[...] = jnp.zeros_like(acc_cur)

        acc_cur[...] += _dot(a_ref[...], w_ref[...])
        xn = _post_ln_epilogue(acc_prev[rows, :], x_ref[rows, :], m_ref[...], 5, g_ref[...], b_ref[...], alpha)
        xo_ref[rows, :] = xn
        if with_next:
            mn = mn_ref[...]
            h_ref[rows, :] = (_ln(xn, ADA_EPS) * (1.0 + mn[1:2]) + mn[0:1]).astype(BF16)

    pl.when(i % 2 == 0)(lambda: step(acc0_ref, acc1_ref))
    pl.when(i % 2 == 1)(lambda: step(acc1_ref, acc0_ref))


def _down_ln(a, w_down, x, mods, ln_g, ln_b, mods_next, *, rows, seq, nb, alpha):
    d = x.shape[1]
    dff = a.shape[1]
    tm = _tile((512, 256, 128), rows, seq)
    nk = next(c for c in (4, 2, 1, 8, 11, 22, 44) if dff % c == 0 and (dff // c) % LANES == 0)
    tk = dff // nk
    with_next = mods_next is not None
    nt = rows // tm
    assert tm % nk == 0 and (tm // nk) % (2 * SUBLANES) == 0
    lag = lambda i: jnp.maximum(i - 1, 0)
    row = pl.BlockSpec((tm, d), lambda i, k: (lag(i), 0))
    vec = pl.BlockSpec((1, d), lambda i, k: (0, 0))
    mod = pl.BlockSpec((None, 6, d), lambda i, k: (jnp.minimum(lag(i) * tm // seq, nb), 0, 0))
    specs = [pl.BlockSpec((tm, tk), lambda i, k: (jnp.minimum(i, nt - 1), k)),
             pl.BlockSpec((tk, d), lambda i, k: (k, 0)), row, mod, vec, vec]
    ins = [a, w_down, x, mods, ln_g, ln_b]
    out_specs = [row]
    out_shape = [jax.ShapeDtypeStruct((rows, d), F32)]
    if with_next:
        specs.append(mod)
        ins.append(mods_next)
        out_specs.append(row)
        out_shape.append(jax.ShapeDtypeStruct((rows, d), BF16))
    res = pl.pallas_call(
        functools.partial(_down_ln_kernel, alpha=alpha, nk=nk, with_next=with_next),
        grid=(nt + 1, nk), in_specs=specs, out_specs=out_specs, out_shape=out_shape,
        scratch_shapes=[pltpu.VMEM((tm, d), F32), pltpu.VMEM((tm, d), F32)],
        compiler_params=_cp("arbitrary", "arbitrary"), name="down_ln",
    )(*ins)
    return (res[0], res[1]) if with_next else (res[0], None)


def _rope_tables(seq, dims, half):
    t = jnp.arange(seq)
    rows, cols = (t // GRID_W).astype(F32), (t % GRID_W).astype(F32)
    j = jnp.arange(LANES)
    freqs = ROPE_THETA ** (-jnp.arange(half, dtype=F32) / half)
    f = freqs[j % half]
    use_row = (j % dims) < dims // 2
    ang = jnp.where(use_row[None, :], rows[:, None], cols[:, None]) * f[None, :]
    cos, sin = jnp.cos(ang), jnp.sin(ang)
    upper = (j % (2 * half)) < half
    return cos, jnp.where(upper[None, :], -sin, 0.0), jnp.where(upper[None, :], 0.0, sin)


def _rpb_rows(rpb):
    h, nr, nc = rpb.shape
    w = jnp.pad(rpb.astype(F32), ((0, 0), (0, 2 * SUBLANES - nr), (0, LANES - nc)))
    return jnp.roll(w, -(NA_KC - 1), axis=2)


def kernel(x, c, ctx, c_ctx, w_ada, b_ada, w_in, mla_kv_norm, w_mla_ukv, gqa_q_norm, gqa_k_norm, na_rpb,
           w_branch, w_out, ln_a_g, ln_a_b, w_up, conv_w, conv_b, w_down, ln_f_g, ln_f_b):
    nb, seq, d = x.shape
    lctx = ctx.shape[1]
    depth = w_in.shape[0]
    heads = d // HEAD_DIM
    gkv = heads // 4
    group = heads // gkv
    kvr = mla_kv_norm.shape[1]
    nlat, nctx = nb * seq, nb * lctx
    t_all = nlat + nctx
    alpha = float((2 * depth) ** 0.25)
    mla_scale = float((MLA_NOPE + MLA_ROPE) ** -0.5) * LOG2E
    hd_scale = float(HEAD_DIM ** -0.5) * LOG2E
    assert seq % (NA_QROWS * GRID_W) == 0 and seq // (NA_QROWS * GRID_W) >= 3
    assert heads % 4 == 0 and seq % lctx == 0 and nctx <= seq

    o_ckv = heads * (MLA_NOPE + MLA_ROPE)
    o_kr = o_ckv + kvr
    o_na = o_kr + MLA_ROPE
    o_gq = o_na + 3 * heads * HEAD_DIM
    o_gkv = o_gq + heads * HEAD_DIM
    o_gate = o_gkv + 2 * gkv * HEAD_DIM

    pad = (-(nb + 1)) % SUBLANES
    c_all = jnp.concatenate([c, c_ctx[None, :], jnp.zeros((pad, d), F32)], 0)
    mods = _ada(c_all, w_ada, b_ada).reshape(depth, nb + 1 + pad, 6, d)

    tm_proj = _tile((1024, 512, 256, 128), seq, nctx)
    rope_g = _rope_tables(seq, HEAD_DIM, HEAD_DIM // 4)
    rope_m = _rope_tables(seq, MLA_ROPE, MLA_ROPE // 4)

    xs, h = _ln_mod(x.reshape(nlat, d), ctx.reshape(nctx, d), mods[0], seq, nb)

    c_gq, c_naq, c_gate = 0, heads, 4 * heads
    c_mqn = c_gate + 3 * heads
    c_gv = c_mqn + heads
    c_gk = c_gv + gkv
    c_mqr = c_gk + gkv
    ones = lambda nblk: jnp.ones((nblk * HEAD_DIM,), F32)
    cs_proj = jnp.concatenate([ones(heads), ones(heads) * hd_scale, ones(5 * heads), ones(heads) * mla_scale,
                               ones(2 * gkv), ones(heads // 2) * mla_scale])[None, :]

    for l in range(depth):
        last = l == depth - 1
        rows = nlat if last else t_all
        wl = w_in[l]
        mq = wl[:, :o_ckv].reshape(d, heads, MLA_NOPE + MLA_ROPE)
        gkvw = wl[:, o_gkv:o_gate].reshape(d, 2, gkv * HEAD_DIM)
        w_proj = jnp.concatenate(
            [wl[:, o_gq:o_gkv], wl[:, o_na:o_gq], wl[:, o_gate:], mq[:, :, :MLA_NOPE].reshape(d, heads * MLA_NOPE),
             gkvw[:, 1], gkvw[:, 0], mq[:, :, MLA_NOPE:].reshape(d, heads * MLA_ROPE)], 1).astype(BF16)
        w_kr2 = jnp.concatenate([wl[:, o_kr:o_na], wl[:, o_kr:o_na]], 1).astype(BF16)

        p = _mm(h, w_proj, tm=tm_proj, cs=cs_proj, name="proj")
        p_ckv, p_kr = _proj_ckv(h, wl[:, o_ckv:o_kr].astype(BF16), mla_kv_norm[l][None, :], w_kr2, tm=tm_proj)
        p_kv = _mm(p_ckv, w_mla_ukv[l].astype(BF16), tm=tm_proj, name="mla_ukv",
                   tn=_tile((2048, 1024, 512, 256, 128), w_mla_ukv.shape[2]))

        dims = dict(nb=nb, seq=seq, ctx=lctx)
        gains = ((gqa_q_norm[l] * hd_scale)[None, :], gqa_k_norm[l][None, :])
        ya = _mla(p, c_mqn // 2, c_mqr, p_kv, p_kr, heads=heads, with_lat=True, out_rows=nlat, name="mla_lat",
                  tabs=rope_m, **dims)
        yb = _na(p, c_naq, _rpb_rows(na_rpb[l]), heads=heads, out_rows=nlat, **dims)
        yc = _gattn(p, c_gq // group, p, c_gk, p, c_gv, group=group, n_kv=gkv, with_lat=True, out_rows=nlat,
                    name="gqa_lat", gains=gains, tabs=rope_g, **dims)
        ys_ctx = None
        if not last:
            ya_c = _mla(p, c_mqn // 2, c_mqr, p_kv, p_kr, heads=heads, with_lat=False, out_rows=nctx,
                        name="mla_ctx", **dims)
            yb_c = _gattn(p, c_naq, p, c_naq + heads, p, c_naq + 2 * heads, group=1, n_kv=heads, with_lat=False,
                          out_rows=nctx, name="na_ctx", **dims)
            yc_c = _gattn(p, c_gq // group, p, c_gk, p, c_gv, group=group, n_kv=gkv, with_lat=False,
                          out_rows=nctx, name="gqa_ctx", gains=gains, **dims)
            ys_ctx = (ya_c, yb_c, yc_c)

        merged = _merge((ya, yb, yc), ys_ctx, w_branch[l].astype(BF16), p, c_gate * HEAD_DIM, rows=rows)
        xs, h2 = _out_ln(merged, w_out[l].astype(BF16), xs, mods[l], ln_a_g[l][None, :], ln_a_b[l][None, :],
                         rows=rows, seq=seq, nb=nb, alpha=alpha)
        hid = _ffn_up(h2, w_up[l].astype(BF16), conv_w[l], conv_b[l][None, :], rows=rows, seq=seq, ctx=lctx,
                      nlat=nlat)
        xs, h = _down_ln(hid, w_down[l].astype(BF16), xs, mods[l], ln_f_g[l][None, :], ln_f_b[l][None, :],
                         None if last else mods[l + 1], rows=rows, seq=seq, nb=nb, alpha=alpha)
    return xs[:nlat].reshape(nb, seq, d)
```

```python
import functools
import math

import jax
import jax.numpy as jnp
from jax import lax
from jax.experimental import pallas as pl
from jax.experimental.pallas import tpu as pltpu

F32 = jnp.float32
BF16 = jnp.bfloat16

GRID_W = 64
HEAD_DIM = 128
MLA_NOPE = 128
MLA_ROPE = 64
NA_KR = 8
NA_KC = 16
NA_QROWS = 4
NA_WROWS = NA_QROWS + NA_KR
ROPE_THETA = 10000.0
ADA_EPS = 1e-6
POST_EPS = 1e-5
RMS_EPS = 1e-6
NEG_INF = -1e30
LOG2E = math.log2(math.e)
LANES = 128
SUBLANES = 8
VMEM_LIMIT = 56 * 2**20
ATTN_TQ = 256


def _cp(*sem):
    return pltpu.CompilerParams(dimension_semantics=sem, vmem_limit_bytes=VMEM_LIMIT)


def _tile(cands, *ns):
    for c in cands:
        if all(n % c == 0 for n in ns):
            return c
    raise ValueError(f"no tile in {cands} divides {ns}")


def _ln(x, eps):
    mu = jnp.mean(x, -1, keepdims=True)
    xc = x - mu
    var = jnp.mean(xc * xc, -1, keepdims=True)
    return xc * lax.rsqrt(var + eps)


def _rms(x):
    return x * lax.rsqrt(jnp.mean(x * x, -1, keepdims=True) + RMS_EPS)


def _rope(x, tabs, shift):
    cos, sa, sb = tabs
    return x * cos + pltpu.roll(x, LANES - shift, 1) * sa + pltpu.roll(x, shift, 1) * sb


def _dot(a, b):
    return jnp.dot(a, b, preferred_element_type=F32)


def _dot_nt(a, b):
    return lax.dot_general(a, b, (((1,), (1,)), ((), ())), preferred_element_type=F32)


def _ada_kernel(c_ref, w_ref, b_ref, o_ref):
    c = c_ref[...]
    a = (c * jax.nn.sigmoid(c)).astype(BF16)
    o_ref[...] = _dot(a, w_ref[...].astype(BF16)) + b_ref[...]


def _ada(c_all, w_ada, b_ada):
    depth, d, n = w_ada.shape
    r = c_all.shape[0]
    tn = _tile((1024, 512, 256, 128), n)
    return pl.pallas_call(
        _ada_kernel,
        grid=(depth, n // tn),
        in_specs=[pl.BlockSpec((r, d), lambda l, j: (0, 0)),
                  pl.BlockSpec((None, d, tn), lambda l, j: (l, 0, j)),
                  pl.BlockSpec((None, 1, tn), lambda l, j: (l, 0, j))],
        out_specs=pl.BlockSpec((None, r, tn), lambda l, j: (l, 0, j)),
        out_shape=jax.ShapeDtypeStruct((depth, r, n), F32),
        compiler_params=_cp("parallel", "parallel"),
        name="ada",
    )(c_all, w_ada, b_ada.reshape(depth, 1, n))


def _ln_mod_kernel(x_ref, c_ref, m_ref, xs_ref, h_ref, *, nlat_tiles):
    m = m_ref[...]

    def emit(src_ref):
        x = src_ref[...]
        xs_ref[...] = x
        h_ref[...] = (_ln(x, ADA_EPS) * (1.0 + m[1:2]) + m[0:1]).astype(BF16)

    pl.when(pl.program_id(0) < nlat_tiles)(lambda: emit(x_ref))
    pl.when(pl.program_id(0) >= nlat_tiles)(lambda: emit(c_ref))


def _ln_mod(x, ctx, mods, seq, nb):
    nlat, d = x.shape
    nctx = ctx.shape[0]
    tm = _tile((512, 256, 128), seq, nctx)
    nl = nlat // tm
    row = pl.BlockSpec((tm, d), lambda i: (i, 0))
    return pl.pallas_call(
        functools.partial(_ln_mod_kernel, nlat_tiles=nl),
        grid=((nlat + nctx) // tm,),
        in_specs=[pl.BlockSpec((tm, d), lambda i: (jnp.minimum(i, nl - 1), 0)),
                  pl.BlockSpec((tm, d), lambda i: (jnp.maximum(i - nl, 0), 0)),
                  pl.BlockSpec((None, 6, d), lambda i: (jnp.minimum(i * tm // seq, nb), 0, 0))],
        out_specs=[row, row],
        out_shape=[jax.ShapeDtypeStruct((nlat + nctx, d), F32), jax.ShapeDtypeStruct((nlat + nctx, d), BF16)],
        compiler_params=_cp("arbitrary"),
        name="ln_mod",
    )(x, ctx, mods)


def _mm_kernel(a_ref, w_ref, o_ref):
    o_ref[...] = _dot(a_ref[...], w_ref[...]).astype(o_ref.dtype)


def _mm_scale_kernel(a_ref, w_ref, cs_ref, o_ref):
    o_ref[...] = (_dot(a_ref[...], w_ref[...]) * cs_ref[...]).astype(o_ref.dtype)


def _mm(a, w, *, rows=None, tm=None, tn=None, cs=None, name="mm"):
    t, k = a.shape
    n = w.shape[1]
    rows = t if rows is None else rows
    tm = tm or _tile((1024, 512, 256, 128), rows)
    tn = tn or _tile((1024, 512, 256, 128), n)
    a_spec = pl.BlockSpec((tm, k), lambda i, j: (i, 0))
    w_spec = pl.BlockSpec((k, tn), lambda i, j: (0, j))
    if cs is not None:
        kern, ins, specs = _mm_scale_kernel, (a, w, cs), [a_spec, w_spec, pl.BlockSpec((1, tn), lambda i, j: (0, j))]
    else:
        kern, ins, specs = _mm_kernel, (a, w), [a_spec, w_spec]
    return pl.pallas_call(
        kern, grid=(rows // tm, n // tn), in_specs=specs,
        out_specs=pl.BlockSpec((tm, tn), lambda i, j: (i, j)),
        out_shape=jax.ShapeDtypeStruct((rows, n), BF16),
        compiler_params=_cp("parallel", "parallel"), name=name,
    )(*ins)


def _ckv_kernel(a_ref, wc_ref, g_ref, wr_ref, oc_ref, or_ref):
    a = a_ref[...]
    oc_ref[...] = (_rms(_dot(a, wc_ref[...])) * g_ref[...]).astype(oc_ref.dtype)
    or_ref[...] = _dot(a, wr_ref[...]).astype(or_ref.dtype)


def _proj_ckv(a, w_ckv, gain, w_kr2, *, tm):
    t, k = a.shape
    r = w_ckv.shape[1]
    full = lambda shape: pl.BlockSpec(shape, lambda i: (0, 0))
    return pl.pallas_call(
        _ckv_kernel, grid=(t // tm,),
        in_specs=[pl.BlockSpec((tm, k), lambda i: (i, 0)), full((k, r)), full((1, r)), full((k, LANES))],
        out_specs=[pl.BlockSpec((tm, r), lambda i: (i, 0)), pl.BlockSpec((tm, LANES), lambda i: (i, 0))],
        out_shape=[jax.ShapeDtypeStruct((t, r), BF16), jax.ShapeDtypeStruct((t, LANES), BF16)],
        compiler_params=_cp("parallel"), name="proj_ckv",
    )(a, w_ckv, gain, w_kr2)


def _softmax_pv(s, v):
    e = jnp.exp2(s - jnp.max(s, -1, keepdims=True))
    den = jnp.sum(e, -1, keepdims=True)
    return _dot(e.astype(BF16), v) * (1.0 / den)


def _softmax_pv_ones(s, v_ext):
    e = jnp.exp2(s - jnp.max(s, -1, keepdims=True)).astype(BF16)
    o = _dot(e, v_ext)
    return o[:, :HEAD_DIM] * (1.0 / o[:, HEAD_DIM:HEAD_DIM + 1])


def _stage_values(vf_ref, v_lat, v_ctx):
    nl = v_lat.shape[0]
    vf_ref[:nl, :HEAD_DIM] = v_lat
    vf_ref[nl:, :HEAD_DIM] = v_ctx
    lane = lax.broadcasted_iota(jnp.int32, (vf_ref.shape[0], LANES), 1)
    vf_ref[:, HEAD_DIM:] = jnp.where(lane == 0, 1.0, 0.0).astype(vf_ref.dtype)


def _chunk_rows(i, tq):
    return pl.ds(i * tq, tq) if isinstance(i, int) else pl.ds(pl.multiple_of(i * tq, tq), tq)


def _pipelined_chunks(nq, qk, sm_pv):
    qk(0, 0)
    for i in range(nq):
        if i + 1 < nq:
            qk(i + 1, (i + 1) % 2)
        sm_pv(i, i % 2)


def _gattn_lat_kernel(q_ref, kl_ref, vl_ref, kc_ref, vc_ref, qg_ref, kg_ref, cos_ref, sa_ref, sb_ref,
                      o_ref, kf_ref, vf_ref, s0_ref, s1_ref, *, group, tq):
    nl = kl_ref.shape[0]

    def prep(x, g_ref, tabs):
        x = _rms(x.astype(F32)) * g_ref[...]
        if tabs is not None:
            x = _rope(x, tabs, HEAD_DIM // 4)
        return x.astype(BF16)

    kf_ref[:nl, :] = prep(kl_ref[...], kg_ref, (cos_ref[...], sa_ref[...], sb_ref[...]))
    kf_ref[nl:, :] = prep(kc_ref[...], kg_ref, None)
    _stage_values(vf_ref, vl_ref[...], vc_ref[...])

    s_refs = (s0_ref, s1_ref)

    def qk(i, slot):
        rows = _chunk_rows(i, tq)
        tabs = (cos_ref[rows, :], sa_ref[rows, :], sb_ref[rows, :])
        for g in range(group):
            q = prep(q_ref[rows, g * HEAD_DIM:(g + 1) * HEAD_DIM], qg_ref, tabs)
            s_refs[slot][g] = _dot_nt(q, kf_ref[...])

    def sm_pv(i, slot):
        rows = _chunk_rows(i, tq)
        for g in range(group):
            o = _softmax_pv_ones(s_refs[slot][g], vf_ref[...])
            o_ref[rows, g * HEAD_DIM:(g + 1) * HEAD_DIM] = o.astype(o_ref.dtype)

    _pipelined_chunks(nl // tq, qk, sm_pv)


def _gattn_ctx_kernel(*refs, group, rms):
    if rms:
        q_ref, kc_ref, vc_ref, qg_ref, kg_ref, o_ref = refs
        norm = lambda x, g_ref: (_rms(x.astype(F32)) * g_ref[...]).astype(BF16)
        k = norm(kc_ref[...], kg_ref)
    else:
        q_ref, kc_ref, vc_ref, o_ref = refs
        k = kc_ref[...]
    for g in range(group):
        sl = slice(g * HEAD_DIM, (g + 1) * HEAD_DIM)
        q = norm(q_ref[:, sl], qg_ref) if rms else q_ref[:, sl]
        o_ref[:, sl] = _softmax_pv(_dot_nt(q, k), vc_ref[...]).astype(o_ref.dtype)


def _gattn(q_arr, q_col, k_arr, k_col, v_arr, v_col, *, group, n_kv, nb, seq, ctx, with_lat, out_rows, name,
           gains=None, tabs=None):
    nlat_c = nb * seq // ctx
    gw = group * HEAD_DIM
    kc_spec = pl.BlockSpec((ctx, HEAD_DIM), lambda b, h: (nlat_c + b, k_col + h))
    vc_spec = pl.BlockSpec((ctx, HEAD_DIM), lambda b, h: (nlat_c + b, v_col + h))
    g_spec = pl.BlockSpec((1, HEAD_DIM), lambda b, h: (0, 0))
    if with_lat:
        tq = _tile((ATTN_TQ, 128), seq)
        t_spec = pl.BlockSpec((seq, LANES), lambda b, h: (0, 0))
        specs = [pl.BlockSpec((seq, gw), lambda b, h: (b, q_col + h)),
                 pl.BlockSpec((seq, HEAD_DIM), lambda b, h: (b, k_col + h)),
                 pl.BlockSpec((seq, HEAD_DIM), lambda b, h: (b, v_col + h)),
                 kc_spec, vc_spec, g_spec, g_spec, t_spec, t_spec, t_spec]
        ins = [q_arr, k_arr, v_arr, k_arr, v_arr, *gains, *tabs]
        kern = functools.partial(_gattn_lat_kernel, group=group, tq=tq)
        scratch = [pltpu.VMEM((seq + ctx, HEAD_DIM), BF16), pltpu.VMEM((seq + ctx, 2 * HEAD_DIM), BF16),
                   pltpu.VMEM((group, tq, seq + ctx), F32), pltpu.VMEM((group, tq, seq + ctx), F32)]
        o_spec = pl.BlockSpec((seq, gw), lambda b, h: (b, h))
    else:
        rms = gains is not None
        specs = [pl.BlockSpec((ctx, gw), lambda b, h: (nlat_c + b, q_col + h)), kc_spec, vc_spec]
        ins = [q_arr, k_arr, v_arr]
        if rms:
            specs += [g_spec, g_spec]
            ins += list(gains)
        kern = functools.partial(_gattn_ctx_kernel, group=group, rms=rms)
        scratch = []
        o_spec = pl.BlockSpec((ctx, gw), lambda b, h: (b, h))
    return pl.pallas_call(
        kern, grid=(nb, n_kv), in_specs=specs, out_specs=o_spec,
        out_shape=jax.ShapeDtypeStruct((out_rows, n_kv * gw), BF16),
        scratch_shapes=scratch,
        compiler_params=_cp("parallel", "parallel"), name=name,
    )(*ins)


def _mla_q(qn, qr, hh, first_half):
    own = first_half if hh == 0 else jnp.logical_not(first_half)
    return jnp.concatenate([qn, jnp.where(own, qr, 0.0).astype(BF16)], axis=1)


def _mla_lat_kernel(qn_ref, qr_ref, kvl_ref, krl_ref, kvc_ref, krc_ref, cos_ref, sa_ref, sb_ref,
                    o_ref, kf_ref, vf_ref, s0_ref, s1_ref, *, tq):
    nl = kvl_ref.shape[0]
    kr = _rope(krl_ref[...].astype(F32), (cos_ref[...], sa_ref[...], sb_ref[...]), MLA_ROPE // 4).astype(BF16)
    for hh in range(2):
        c0 = hh * 2 * HEAD_DIM
        kf_ref[hh, :nl, :MLA_NOPE] = kvl_ref[:, c0:c0 + MLA_NOPE]
        kf_ref[hh, :nl, MLA_NOPE:] = kr
        kf_ref[hh, nl:, :MLA_NOPE] = kvc_ref[:, c0:c0 + MLA_NOPE]
        kf_ref[hh, nl:, MLA_NOPE:] = krc_ref[...]
        _stage_values(vf_ref.at[hh], kvl_ref[:, c0 + MLA_NOPE:c0 + 2 * HEAD_DIM],
                      kvc_ref[:, c0 + MLA_NOPE:c0 + 2 * HEAD_DIM])
    first_half = lax.broadcasted_iota(jnp.int32, (tq, LANES), 1) < MLA_ROPE

    s_refs = (s0_ref, s1_ref)

    def qk(i, slot):
        rows = _chunk_rows(i, tq)
        qr = _rope(qr_ref[rows, :].astype(F32), (cos_ref[rows, :], sa_ref[rows, :], sb_ref[rows, :]), MLA_ROPE // 4)
        for hh in range(2):
            q = _mla_q(qn_ref[rows, hh * MLA_NOPE:(hh + 1) * MLA_NOPE], qr, hh, first_half)
            s_refs[slot][hh] = _dot_nt(q, kf_ref[hh])

    def sm_pv(i, slot):
        rows = _chunk_rows(i, tq)
        for hh in range(2):
            o = _softmax_pv_ones(s_refs[slot][hh], vf_ref[hh])
            o_ref[rows, hh * HEAD_DIM:(hh + 1) * HEAD_DIM] = o.astype(o_ref.dtype)

    _pipelined_chunks(nl // tq, qk, sm_pv)


def _mla_ctx_kernel(qn_ref, qr_ref, kvc_ref, krc_ref, o_ref):
    qr = qr_ref[...].astype(F32)
    first_half = lax.broadcasted_iota(jnp.int32, qr.shape, 1) < MLA_ROPE
    for hh in range(2):
        c0 = hh * 2 * HEAD_DIM
        q = _mla_q(qn_ref[:, hh * MLA_NOPE:(hh + 1) * MLA_NOPE], qr, hh, first_half)
        k = jnp.concatenate([kvc_ref[:, c0:c0 + MLA_NOPE], krc_ref[...]], axis=1)
        v = kvc_ref[:, c0 + MLA_NOPE:c0 + 2 * HEAD_DIM]
        o_ref[:, hh * HEAD_DIM:(hh + 1) * HEAD_DIM] = _softmax_pv(_dot_nt(q, k), v).astype(o_ref.dtype)


def _mla(p_arr, qn_col, qr_col, kv_arr, kr_arr, *, heads, nb, seq, ctx, with_lat, out_rows, name, tabs=None):
    nlat_c = nb * seq // ctx
    qrow = (lambda b: b) if with_lat else (lambda b: nlat_c + b)
    qlen = seq if with_lat else ctx
    specs = [pl.BlockSpec((qlen, 2 * MLA_NOPE), lambda b, h: (qrow(b), qn_col + h)),
             pl.BlockSpec((qlen, 2 * MLA_ROPE), lambda b, h: (qrow(b), qr_col + h))]
    ins = [p_arr, p_arr]
    if with_lat:
        specs += [pl.BlockSpec((seq, 4 * HEAD_DIM), lambda b, h: (b, h)),
                  pl.BlockSpec((seq, LANES), lambda b, h: (b, 0))]
        ins += [kv_arr, kr_arr]
    specs += [pl.BlockSpec((ctx, 4 * HEAD_DIM), lambda b, h: (nlat_c + b, h)),
              pl.BlockSpec((ctx, LANES), lambda b, h: (nlat_c + b, 0))]
    ins += [kv_arr, kr_arr]
    if with_lat:
        tq = _tile((2 * ATTN_TQ, ATTN_TQ, 128), seq)
        specs += [pl.BlockSpec((seq, LANES), lambda b, h: (0, 0))] * 3
        ins += list(tabs)
        kern = functools.partial(_mla_lat_kernel, tq=tq)
        scratch = [pltpu.VMEM((2, seq + ctx, 2 * HEAD_DIM), BF16), pltpu.VMEM((2, seq + ctx, 2 * HEAD_DIM), BF16),
                   pltpu.VMEM((2, tq, seq + ctx), F32), pltpu.VMEM((2, tq, seq + ctx), F32)]
    else:
        kern, scratch = _mla_ctx_kernel, []
    return pl.pallas_call(
        kern, grid=(nb, heads // 2), in_specs=specs,
        out_specs=pl.BlockSpec((qlen, 2 * HEAD_DIM), lambda b, h: (b, h)),
        out_shape=jax.ShapeDtypeStruct((out_rows, heads * HEAD_DIM), BF16),
        scratch_shapes=scratch,
        compiler_params=_cp("parallel", "parallel"), name=name,
    )(*ins)


def _na_fill_bias(w_ref, bias_ref):
    cq = lax.broadcasted_iota(jnp.int32, (GRID_W, LANES), 0)
    lane = lax.broadcasted_iota(jnp.int32, (GRID_W, LANES), 1)
    ck = lane % GRID_W
    cs = jnp.clip(cq - NA_KC // 2, 0, GRID_W - NA_KC)
    col_ok = (ck >= cs) & (ck < cs + NA_KC)
    even = lane < GRID_W
    ninf = jnp.full((GRID_W, LANES), NEG_INF, F32)
    w = w_ref[...] * LOG2E
    toe = {}

    def toeplitz(a, odd):
        if (a, odd) not in toe:
            wa = jnp.broadcast_to(w[a:a + 1, :], (GRID_W, LANES))
            toe[(a, odd)] = pltpu.roll(wa, GRID_W if odd else 0, 1, stride=1, stride_axis=0)
        return toe[(a, odd)]

    for kind, r0_rel in enumerate((0, NA_KR // 2, NA_WROWS - NA_QROWS)):
        for rq in range(NA_QROWS):
            rs_rel = (0, rq, NA_WROWS - NA_KR)[kind]
            for pair in range(NA_WROWS // 2):
                halves = []
                for odd in (0, 1):
                    rk = 2 * pair + odd
                    if rs_rel <= rk < rs_rel + NA_KR:
                        halves.append(toeplitz(rk - r0_rel - rq + NA_KR - 1, odd))
                    else:
                        halves.append(ninf)
                blk = jnp.where(col_ok, jnp.where(even, halves[0], halves[1]), NEG_INF)
                bias_ref[kind, rq * GRID_W:(rq + 1) * GRID_W, pair * LANES:(pair + 1) * LANES] = blk


def _na_kernel(q_ref, k_ref, v_ref, kc_ref, vc_ref, w_ref, o_ref, bias_ref, *, nblk):
    qt = NA_QROWS * GRID_W
    wt = NA_WROWS * GRID_W

    @pl.when(pl.program_id(1) == 0)
    def _():
        _na_fill_bias(w_ref, bias_ref)

    kc, vc = kc_ref[...], vc_ref[...]
    win = lambda rb: slice(min(max(rb - 1, 0), nblk - 3) * qt, min(max(rb - 1, 0), nblk - 3) * qt + wt)

    def scores(rb):
        q = q_ref[rb * qt:(rb + 1) * qt, :]
        kind = 0 if rb == 0 else (2 if rb == nblk - 1 else 1)
        return _dot_nt(q, k_ref[win(rb), :]) + bias_ref[kind], _dot_nt(q, kc)

    nxt = scores(0)
    for rb in range(nblk):
        s_lat, s_ctx = nxt
        if rb + 1 < nblk:
            nxt = scores(rb + 1)
        m = jnp.maximum(jnp.max(s_lat, -1, keepdims=True), jnp.max(s_ctx, -1, keepdims=True))
        e_lat, e_ctx = jnp.exp2(s_lat - m), jnp.exp2(s_ctx - m)
        den = jnp.sum(e_lat, -1, keepdims=True) + jnp.sum(e_ctx, -1, keepdims=True)
        o = _dot(e_lat.astype(BF16), v_ref[win(rb), :]) + _dot(e_ctx.astype(BF16), vc)
        o_ref[rb * qt:(rb + 1) * qt, :] = (o * (1.0 / den)).astype(o_ref.dtype)


def _na(p_arr, q_col, rpb_w, *, heads, nb, seq, ctx, out_rows):
    nlat_c = nb * seq // ctx
    nblk = seq // (NA_QROWS * GRID_W)
    qt, wt = NA_QROWS * GRID_W, NA_WROWS * GRID_W
    lat = lambda off: pl.BlockSpec((seq, HEAD_DIM), lambda h, b: (b, q_col + off + h))
    cx = lambda off: pl.BlockSpec((ctx, HEAD_DIM), lambda h, b: (nlat_c + b, q_col + off + h))
    return pl.pallas_call(
        functools.partial(_na_kernel, nblk=nblk),
        grid=(heads, nb),
        in_specs=[lat(0), lat(heads), lat(2 * heads), cx(heads), cx(2 * heads),
                  pl.BlockSpec((None,) + rpb_w.shape[1:], lambda h, b: (h, 0, 0))],
        out_specs=pl.BlockSpec((seq, HEAD_DIM), lambda h, b: (b, h)),
        out_shape=jax.ShapeDtypeStruct((out_rows, heads * HEAD_DIM), BF16),
        scratch_shapes=[pltpu.VMEM((3, qt, wt), F32)],
        compiler_params=_cp("parallel", "arbitrary"), name="na_lat",
    )(p_arr, p_arr, p_arr, p_arr, p_arr, rpb_w)


def _merge_kernel(*refs, nlat_tiles):
    if nlat_tiles is None:
        ya_ref, yb_ref, yc_ref, w_ref, ga_ref, gb_ref, gc_ref, o_ref = refs
        ctx_refs = None
    else:
        ya_ref, yb_ref, yc_ref, ca_ref, cb_ref, cc_ref, w_ref, ga_ref, gb_ref, gc_ref, o_ref = refs
        ctx_refs = (ca_ref, cb_ref, cc_ref)

    def emit(y_refs):
        acc = None
        for i, (y_ref, g_ref) in enumerate(zip(y_refs, (ga_ref, gb_ref, gc_ref))):
            t = jax.nn.sigmoid(g_ref[...].astype(F32)) * _dot(y_ref[...], w_ref[i])
            acc = t if acc is None else acc + t
        o_ref[...] = acc.astype(o_ref.dtype)

    if ctx_refs is None:
        emit((ya_ref, yb_ref, yc_ref))
    else:
        pl.when(pl.program_id(0) < nlat_tiles)(lambda: emit((ya_ref, yb_ref, yc_ref)))
        pl.when(pl.program_id(0) >= nlat_tiles)(lambda: emit(ctx_refs))


def _merge(ys_lat, ys_ctx, w_branch, p_arr, gate_col, *, rows):
    nlat, d = ys_lat[0].shape
    n = w_branch.shape[2]
    tm = _tile((512, 256, 128), nlat, rows)
    tn = _tile((512, 256, 128), n)
    nl = nlat // tm
    g_spec = lambda b: pl.BlockSpec((tm, tn), lambda i, j: (i, (gate_col + b * n) // tn + j))
    specs = [pl.BlockSpec((tm, d), lambda i, j: (jnp.minimum(i, nl - 1), 0))] * 3
    ins = list(ys_lat)
    if ys_ctx is not None:
        specs += [pl.BlockSpec((tm, d), lambda i, j: (jnp.maximum(i - nl, 0), 0))] * 3
        ins += list(ys_ctx)
    specs += [pl.BlockSpec((3, d, tn), lambda i, j: (0, 0, j)), g_spec(0), g_spec(1), g_spec(2)]
    ins += [w_branch, p_arr, p_arr, p_arr]
    return pl.pallas_call(
        functools.partial(_merge_kernel, nlat_tiles=None if ys_ctx is None else nl),
        grid=(rows // tm, n // tn), in_specs=specs,
        out_specs=pl.BlockSpec((tm, tn), lambda i, j: (i, j)),
        out_shape=jax.ShapeDtypeStruct((rows, n), BF16),
        compiler_params=_cp("parallel", "parallel"), name="merge",
    )(*ins)


def _post_ln_epilogue(y, x, m, gate_row, g, b, alpha):
    z = alpha * x + m[gate_row:gate_row + 1] * y
    return _ln(z, POST_EPS) * g + b


def _out_ln_kernel(a_ref, w_ref, x_ref, m_ref, g_ref, b_ref, xo_ref, h_ref, y0_ref, y1_ref, *, alpha):
    i = pl.program_id(0)

    @pl.when(i == 0)
    def _():
        y1_ref[...] = jnp.zeros_like(y1_ref)

    def step(y_cur, y_prev):
        y_cur[...] = _dot(a_ref[...], w_ref[...])
        m = m_ref[...]
        xn = _post_ln_epilogue(y_prev[...], x_ref[...], m, 2, g_ref[...], b_ref[...], alpha)
        xo_ref[...] = xn
        h_ref[...] = (_ln(xn, ADA_EPS) * (1.0 + m[4:5]) + m[3:4]).astype(BF16)

    pl.when(i % 2 == 0)(lambda: step(y0_ref, y1_ref))
    pl.when(i % 2 == 1)(lambda: step(y1_ref, y0_ref))


def _out_ln(a, w_out, x, mods, ln_g, ln_b, *, rows, seq, nb, alpha):
    d = x.shape[1]
    tm = _tile((256, 128), rows, seq)
    nt = rows // tm
    lag = lambda i: jnp.maximum(i - 1, 0)
    row = pl.BlockSpec((tm, d), lambda i: (lag(i), 0))
    vec = pl.BlockSpec((1, d), lambda i: (0, 0))
    return pl.pallas_call(
        functools.partial(_out_ln_kernel, alpha=alpha), grid=(nt + 1,),
        in_specs=[pl.BlockSpec((tm, d), lambda i: (jnp.minimum(i, nt - 1), 0)),
                  pl.BlockSpec((d, d), lambda i: (0, 0)), row,
                  pl.BlockSpec((None, 6, d), lambda i: (jnp.minimum(lag(i) * tm // seq, nb), 0, 0)), vec, vec],
        out_specs=[row, row],
        out_shape=[jax.ShapeDtypeStruct((rows, d), F32), jax.ShapeDtypeStruct((rows, d), BF16)],
        scratch_shapes=[pltpu.VMEM((tm, d), F32), pltpu.VMEM((tm, d), F32)],
        compiler_params=_cp("arbitrary"), name="out_ln",
    )(a, w_out, x, mods, ln_g, ln_b)


def _seq_edge_rows(base, tm, seq, ctx, nlat):
    in_lat = base < nlat
    length = jnp.where(in_lat, seq, ctx)
    pos = jnp.where(in_lat, base % seq, (base - nlat) % ctx) + lax.broadcasted_iota(jnp.int32, (tm, 1), 0)
    first, last = pos == 0, pos == length - 1
    for k in range(1, tm // min(seq, ctx) + 1):
        first, last = first | (pos == k * length), last | (pos == k * length + length - 1)
    return first, last


def _ffn_up_kernel(a_ref, ap_ref, an_ref, wg_ref, wv_ref, cw_ref, cb_ref, o_ref, aext_ref, buf_ref,
                   *, tm, seq, ctx, nlat):
    halo = ap_ref.shape[0]

    @pl.when(pl.program_id(1) == 0)
    def _():
        aext_ref[:halo, :] = ap_ref[...]
        aext_ref[halo:halo + tm, :] = a_ref[...]
        aext_ref[halo + tm:, :] = an_ref[...]

    buf_ref[...] = _dot(aext_ref[...], wg_ref[...])
    val = _dot(a_ref[...], wv_ref[...])
    first, last = _seq_edge_rows(pl.program_id(0) * tm, tm, seq, ctx, nlat)
    cw = cw_ref[...]
    g_prev = jnp.where(first, 0.0, buf_ref[halo - 1:halo - 1 + tm, :])
    g_next = jnp.where(last, 0.0, buf_ref[halo + 1:halo + 1 + tm, :])
    gc = g_prev * cw[0:1] + buf_ref[halo:halo + tm, :] * cw[1:2] + g_next * cw[2:3] + cb_ref[...]
    o_ref[...] = (gc * jax.nn.sigmoid(gc) * val).astype(o_ref.dtype)


def _ffn_up(a, w_up, conv_w, conv_b, *, rows, seq, ctx, nlat):
    k = a.shape[1]
    dff = conv_w.shape[1]
    halo = 2 * SUBLANES
    tm = _tile((1024, 512, 256, 128), rows, seq)
    tf = _tile((512, 256, 128), dff)
    nf = dff // tf
    hb = tm // halo
    last_hb = rows // halo - 1
    return pl.pallas_call(
        functools.partial(_ffn_up_kernel, tm=tm, seq=seq, ctx=ctx, nlat=nlat),
        grid=(rows // tm, nf),
        in_specs=[pl.BlockSpec((tm, k), lambda i, j: (i, 0)),
                  pl.BlockSpec((halo, k), lambda i, j: (jnp.maximum(i * hb - 1, 0), 0)),
                  pl.BlockSpec((halo, k), lambda i, j: (jnp.minimum((i + 1) * hb, last_hb), 0)),
                  pl.BlockSpec((k, tf), lambda i, j: (0, j)),
                  pl.BlockSpec((k, tf), lambda i, j: (0, nf + j)),
                  pl.BlockSpec((3, tf), lambda i, j: (0, j)),
                  pl.BlockSpec((1, tf), lambda i, j: (0, j))],
        out_specs=pl.BlockSpec((tm, tf), lambda i, j: (i, j)),
        out_shape=jax.ShapeDtypeStruct((rows, dff), BF16),
        scratch_shapes=[pltpu.VMEM((tm + 2 * halo, k), BF16), pltpu.VMEM((tm + 2 * halo, tf), F32)],
        compiler_params=_cp("parallel", "arbitrary"), name="ffn_up",
    )(a, a, a, w_up, w_up, conv_w, conv_b)


def _down_ln_kernel(*refs, alpha, nk, with_next):
    if with_next:
        a_ref, w_ref, x_ref, m_ref, g_ref, b_ref, mn_ref, xo_ref, h_ref, acc0_ref, acc1_ref = refs
    else:
        a_ref, w_ref, x_ref, m_ref, g_ref, b_ref, xo_ref, acc0_ref, acc1_ref = refs
    i, k = pl.program_id(0), pl.program_id(1)
    part = acc0_ref.shape[0] // nk
    rows = pl.ds(pl.multiple_of(k * part, part), part)

    @pl.when((i == 0) & (k == 0))
    def _():
        acc1_ref[...] = jnp.zeros_like(acc1_ref)

    def step(acc_cur, acc_prev):
        @pl.when(k == 0)
        def _():
            acc_cur[...] = jnp.zeros_like(acc_cur)

        acc_cur[...] += _dot(a_ref[...], w_ref[...])
        xn = _post_ln_epilogue(acc_prev[rows, :], x_ref[rows, :], m_ref[...], 5, g_ref[...], b_ref[...], alpha)
        xo_ref[rows, :] = xn
        if with_next:
            mn = mn_ref[...]
            h_ref[rows, :] = (_ln(xn, ADA_EPS) * (1.0 + mn[1:2]) + mn[0:1]).astype(BF16)

    pl.when(i % 2 == 0)(lambda: step(acc0_ref, acc1_ref))
    pl.when(i % 2 == 1)(lambda: step(acc1_ref, acc0_ref))


def _down_ln(a, w_down, x, mods, ln_g, ln_b, mods_next, *, rows, seq, nb, alpha):
    d = x.shape[1]
    dff = a.shape[1]
    tm = _tile((512, 256, 128), rows, seq)
    nk = next(c for c in (4, 2, 1, 8, 11, 22, 44) if dff % c == 0 and (dff // c) % LANES == 0)
    tk = dff // nk
    with_next = mods_next is not None
    nt = rows // tm
    assert tm % nk == 0 and (tm // nk) % (2 * SUBLANES) == 0
    lag = lambda i: jnp.maximum(i - 1, 0)
    row = pl.BlockSpec((tm, d), lambda i, k: (lag(i), 0))
    vec = pl.BlockSpec((1, d), lambda i, k: (0, 0))
    mod = pl.BlockSpec((None, 6, d), lambda i, k: (jnp.minimum(lag(i) * tm // seq, nb), 0, 0))
    specs = [pl.BlockSpec((tm, tk), lambda i, k: (jnp.minimum(i, nt - 1), k)),
             pl.BlockSpec((tk, d), lambda i, k: (k, 0)), row, mod, vec, vec]
    ins = [a, w_down, x, mods, ln_g, ln_b]
    out_specs = [row]
    out_shape = [jax.ShapeDtypeStruct((rows, d), F32)]
    if with_next:
        specs.append(mod)
        ins.append(mods_next)
        out_specs.append(row)
        out_shape.append(jax.ShapeDtypeStruct((rows, d), BF16))
    res = pl.pallas_call(
        functools.partial(_down_ln_kernel, alpha=alpha, nk=nk, with_next=with_next),
        grid=(nt + 1, nk), in_specs=specs, out_specs=out_specs, out_shape=out_shape,
        scratch_shapes=[pltpu.VMEM((tm, d), F32), pltpu.VMEM((tm, d), F32)],
        compiler_params=_cp("arbitrary", "arbitrary"), name="down_ln",
    )(*ins)
    return (res[0], res[1]) if with_next else (res[0], None)


def _rope_tables(seq, dims, half):
    t = jnp.arange(seq)
    rows, cols = (t // GRID_W).astype(F32), (t % GRID_W).astype(F32)
    j = jnp.arange(LANES)
    freqs = ROPE_THETA ** (-jnp.arange(half, dtype=F32) / half)
    f = freqs[j % half]
    use_row = (j % dims) < dims // 2
    ang = jnp.where(use_row[None, :], rows[:, None], cols[:, None]) * f[None, :]
    cos, sin = jnp.cos(ang), jnp.sin(ang)
    upper = (j % (2 * half)) < half
    return cos, jnp.where(upper[None, :], -sin, 0.0), jnp.where(upper[None, :], 0.0, sin)


def _rpb_rows(rpb):
    h, nr, nc = rpb.shape
    w = jnp.pad(rpb.astype(F32), ((0, 0), (0, 2 * SUBLANES - nr), (0, LANES - nc)))
    return jnp.roll(w, -(NA_KC - 1), axis=2)


def kernel(x, c, ctx, c_ctx, w_ada, b_ada, w_in, mla_kv_norm, w_mla_ukv, gqa_q_norm, gqa_k_norm, na_rpb,
           w_branch, w_out, ln_a_g, ln_a_b, w_up, conv_w, conv_b, w_down, ln_f_g, ln_f_b):
    nb, seq, d = x.shape
    lctx = ctx.shape[1]
    depth = w_in.shape[0]
    heads = d // HEAD_DIM
    gkv = heads // 4
    group = heads // gkv
    kvr = mla_kv_norm.shape[1]
    nlat, nctx = nb * seq, nb * lctx
    t_all = nlat + nctx
    alpha = float((2 * depth) ** 0.25)
    mla_scale = float((MLA_NOPE + MLA_ROPE) ** -0.5) * LOG2E
    hd_scale = float(HEAD_DIM ** -0.5) * LOG2E
    assert seq % (NA_QROWS * GRID_W) == 0 and seq // (NA_QROWS * GRID_W) >= 3
    assert heads % 4 == 0 and seq % lctx == 0 and nctx <= seq

    o_ckv = heads * (MLA_NOPE + MLA_ROPE)
    o_kr = o_ckv + kvr
    o_na = o_kr + MLA_ROPE
    o_gq = o_na + 3 * heads * HEAD_DIM
    o_gkv = o_gq + heads * HEAD_DIM
    o_gate = o_gkv + 2 * gkv * HEAD_DIM

    pad = (-(nb + 1)) % SUBLANES
    c_all = jnp.concatenate([c, c_ctx[None, :], jnp.zeros((pad, d), F32)], 0)
    mods = _ada(c_all, w_ada, b_ada).reshape(depth, nb + 1 + pad, 6, d)

    tm_proj = _tile((1024, 512, 256, 128), seq, nctx)
    rope_g = _rope_tables(seq, HEAD_DIM, HEAD_DIM // 4)
    rope_m = _rope_tables(seq, MLA_ROPE, MLA_ROPE // 4)

    xs, h = _ln_mod(x.reshape(nlat, d), ctx.reshape(nctx, d), mods[0], seq, nb)

    c_gq, c_naq, c_gate = 0, heads, 4 * heads
    c_mqn = c_gate + 3 * heads
    c_gv = c_mqn + heads
    c_gk = c_gv + gkv
    c_mqr = c_gk + gkv
    ones = lambda nblk: jnp.ones((nblk * HEAD_DIM,), F32)
    cs_proj = jnp.concatenate([ones(heads), ones(heads) * hd_scale, ones(5 * heads), ones(heads) * mla_scale,
                               ones(2 * gkv), ones(heads // 2) * mla_scale])[None, :]

    for l in range(depth):
        last = l == depth - 1
        rows = nlat if last else t_all
        wl = w_in[l]
        mq = wl[:, :o_ckv].reshape(d, heads, MLA_NOPE + MLA_ROPE)
        gkvw = wl[:, o_gkv:o_gate].reshape(d, 2, gkv * HEAD_DIM)
        w_proj = jnp.concatenate(
            [wl[:, o_gq:o_gkv], wl[:, o_na:o_gq], wl[:, o_gate:], mq[:, :, :MLA_NOPE].reshape(d, heads * MLA_NOPE),
             gkvw[:, 1], gkvw[:, 0], mq[:, :, MLA_NOPE:].reshape(d, heads * MLA_ROPE)], 1).astype(BF16)
        w_kr2 = jnp.concatenate([wl[:, o_kr:o_na], wl[:, o_kr:o_na]], 1).astype(BF16)

        p = _mm(h, w_proj, tm=tm_proj, cs=cs_proj, name="proj")
        p_ckv, p_kr = _proj_ckv(h, wl[:, o_ckv:o_kr].astype(BF16), mla_kv_norm[l][None, :], w_kr2, tm=tm_proj)
        p_kv = _mm(p_ckv, w_mla_ukv[l].astype(BF16), tm=tm_proj, name="mla_ukv",
                   tn=_tile((2048, 1024, 512, 256, 128), w_mla_ukv.shape[2]))

        dims = dict(nb=nb, seq=seq, ctx=lctx)
        gains = ((gqa_q_norm[l] * hd_scale)[None, :], gqa_k_norm[l][None, :])
        ya = _mla(p, c_mqn // 2, c_mqr, p_kv, p_kr, heads=heads, with_lat=True, out_rows=nlat, name="mla_lat",
                  tabs=rope_m, **dims)
        yb = _na(p, c_naq, _rpb_rows(na_rpb[l]), heads=heads, out_rows=nlat, **dims)
        yc = _gattn(p, c_gq // group, p, c_gk, p, c_gv, group=group, n_kv=gkv, with_lat=True, out_rows=nlat,
                    name="gqa_lat", gains=gains, tabs=rope_g, **dims)
        ys_ctx = None
        if not last:
            ya_c = _mla(p, c_mqn // 2, c_mqr, p_kv, p_kr, heads=heads, with_lat=False, out_rows=nctx,
                        name="mla_ctx", **dims)
            yb_c = _gattn(p, c_naq, p, c_naq + heads, p, c_naq + 2 * heads, group=1, n_kv=heads, with_lat=False,
                          out_rows=nctx, name="na_ctx", **dims)
            yc_c = _gattn(p, c_gq // group, p, c_gk, p, c_gv, group=group, n_kv=gkv, with_lat=False,
                          out_rows=nctx, name="gqa_ctx", gains=gains, **dims)
            ys_ctx = (ya_c, yb_c, yc_c)

        merged = _merge((ya, yb, yc), ys_ctx, w_branch[l].astype(BF16), p, c_gate * HEAD_DIM, rows=rows)
        xs, h2 = _out_ln(merged, w_out[l].astype(BF16), xs, mods[l], ln_a_g[l][None, :], ln_a_b[l][None, :],
                         rows=rows, seq=seq, nb=nb, alpha=alpha)
        hid = _ffn_up(h2, w_up[l].astype(BF16), conv_w[l], conv_b[l][None, :], rows=rows, seq=seq, ctx=lctx,
                      nlat=nlat)
        xs, h = _down_ln(hid, w_down[l].astype(BF16), xs, mods[l], ln_f_g[l][None, :], ln_f_b[l][None, :],
                         None if last else mods[l + 1], rows=rows, seq=seq, nb=nb, alpha=alpha)
    return xs[:nlat].reshape(nb, seq, d)
```
